```python
import math
import jax
import jax.numpy as jnp
from jax import lax
import numpy as np

D_MODEL = 4096
BATCH = 4
SEQ = 4096
DEPTH = 1
DEC_BATCH = 2
DEC_SEQ = 8192
PAST_LEN = 128

SSM_WIDTH = D_MODEL // 2
SSM_GROUP_CH = 16
SSM_GROUPS = SSM_WIDTH // SSM_GROUP_CH
SSM_STATE = 64
SSM_DT_MIN = 1e-3
SSM_DT_MAX = 1e-1

HY_WIDTH = D_MODEL // 2
HY_ORDER = 2
HY_SHORT = 3
HY_EMB = 33
HY_FILTER_HIDDEN = 64
HY_SHIFT = 0.05
HY_TARGET = 1e-2
HY_FAST_PCT = 0.3
HY_SLOW_PCT = 1.5

N_EXPERTS = 32
TOP_K = 4
D_FF = D_MODEL
SWIGLU_ALPHA = 1.702
SWIGLU_LIMIT = 7.0
MOE_BLOCK = 256

IN_WIDTH = SSM_WIDTH + (HY_ORDER + 1) * HY_WIDTH + 2 * D_MODEL
DN_ALPHA = (2 * DEPTH) ** 0.25
DN_BETA = (8 * DEPTH) ** -0.25
LN_EPS = 1e-5

kernel_name = 'hybrid_s5_hyena_moe_encoder'

F32 = jnp.float32


def _layer_norm(x, g, b):
    xf = x.astype(F32)
    mu = jnp.mean(xf, axis=-1, keepdims=True)
    var = jnp.mean(jnp.square(xf - mu), axis=-1, keepdims=True)
    return ((xf - mu) * lax.rsqrt(var + LN_EPS) * g.astype(F32) + b.astype(F32)).astype(x.dtype)


def _s5_branch(u, l, p):
    bsz, L, _ = u.shape
    uf = u.astype(F32).reshape(bsz, L, SSM_GROUPS, SSM_GROUP_CH)
    lam = lax.complex(p['ssm_A_re'][l].astype(F32), p['ssm_A_im'][l].astype(F32))
    step = jnp.exp(p['ssm_log_step'][l].astype(F32))[..., None]
    lam_bar = jnp.exp(lam * step)
    b_mat = lax.complex(p['ssm_B_re'][l].astype(F32), p['ssm_B_im'][l].astype(F32))
    b_bar = ((lam_bar - 1.0) / lam)[..., None] * b_mat
    c_mat = lax.complex(p['ssm_C_re'][l].astype(F32), p['ssm_C_im'][l].astype(F32))
    u_dir = jnp.stack([uf, jnp.flip(uf, axis=1)], axis=0)
    bu = jnp.einsum('dblgh,dgph->ldbgp', u_dir, b_bar)
    a = jnp.broadcast_to(lam_bar[None, :, None], (L, 2, 1, SSM_GROUPS, SSM_STATE))

    def combine(left, right):
        a_l, b_l = left
        a_r, b_r = right
        return a_r * a_l, a_r * b_l + b_r

    _, states = lax.associative_scan(combine, (a, bu), axis=0)
    y_dir = jnp.real(jnp.einsum('ldbgp,dghp->dblgh', states, c_mat))
    d_skip = p['ssm_D'][l].astype(F32).reshape(SSM_GROUPS, SSM_GROUP_CH)
    y = (y_dir[0] + jnp.flip(y_dir[1], axis=1) + d_skip * uf).reshape(bsz, L, SSM_WIDTH)
    y = jax.nn.gelu(y)
    y = y * jax.nn.sigmoid(y @ p['ssm_w_glu'][l].astype(F32) + p['ssm_b_glu'][l].astype(F32))
    return y.astype(u.dtype)


def _implicit_filters(L, l, p):
    t = jnp.linspace(0.0, 1.0, L, dtype=F32)[:, None]
    bands = (HY_EMB - 1) // 2
    w = (2.0 * math.pi / L) * jnp.arange(L, dtype=F32)[:, None]
    f = jnp.linspace(1e-4, bands - 1, bands, dtype=F32)[None, :]
    z = jnp.concatenate([t, jnp.cos(f * w), -jnp.sin(f * w)], axis=-1)
    freq = p['hf_freq'][l].astype(F32)
    h = jnp.sin(freq * (z @ p['hf_w1'][l].astype(F32) + p['hf_b1'][l].astype(F32)))
    h = jnp.sin(freq * (h @ p['hf_w2'][l].astype(F32) + p['hf_b2'][l].astype(F32)))
    h = jnp.sin(freq * (h @ p['hf_w3'][l].astype(F32) + p['hf_b3'][l].astype(F32)))
    k = (h @ p['hf_w4'][l].astype(F32)).reshape(L, HY_ORDER, 2, HY_WIDTH)
    decay = jnp.abs(p['hf_decay'][l].astype(F32))
    return k * (jnp.exp(-t[:, :, None, None] * decay) + HY_SHIFT)


def _bidir_fftconv(v, k_fwd, k_bwd, bias):
    L = v.shape[1]
    n_fft = 2 * L
    k_f = jnp.fft.rfft(k_fwd, n=n_fft, axis=0) + jnp.conj(jnp.fft.rfft(k_bwd, n=n_fft, axis=0))
    y = jnp.fft.irfft(jnp.fft.rfft(v, n=n_fft, axis=1) * k_f[None], n=n_fft, axis=1)[:, :L]
    return y + v * bias


def _hyena_branch(u, l, p):
    L = u.shape[1]
    c3 = u.shape[-1]
    w_short = p['hy_short_w'][l].astype(u.dtype)[:, None, :]
    z = lax.conv_general_dilated(u, w_short, (1,), [(1, 1)],
                                 dimension_numbers=('NWC', 'WIO', 'NWC'),
                                 feature_group_count=c3)
    z = z.astype(F32) + p['hy_short_b'][l].astype(F32)
    x1, x2, v = jnp.split(z, 3, axis=-1)
    k = _implicit_filters(L, l, p)
    fb = p['hy_filter_bias'][l].astype(F32)
    y = v
    for n, gate in enumerate((x1, x2)):
        y = gate * _bidir_fftconv(y, k[:, n, 0], k[:, n, 1], fb[n])
    return y.astype(u.dtype)


def _mixer(h, l, p):
    proj = h @ p['w_in'][l]
    c1 = SSM_WIDTH
    c2 = c1 + (HY_ORDER + 1) * HY_WIDTH
    c3 = c2 + D_MODEL
    u_a, u_b, g_a, g_b = jnp.split(proj, [c1, c2, c3], axis=-1)
    y_a = _s5_branch(u_a, l, p) @ p['w_branch_a'][l]
    y_b = _hyena_branch(u_b, l, p) @ p['w_branch_b'][l]
    merged = jax.nn.sigmoid(g_a) * y_a + jax.nn.sigmoid(g_b) * y_b
    return merged @ p['w_out'][l]


def _moe(h, l, p):
    bsz, L, d = h.shape
    n_tok = bsz * L
    hf = h.reshape(n_tok, d)
    logits = (hf @ p['mo_w_router'][l]).astype(F32) + p['mo_b_router'][l].astype(F32)
    top_val, top_idx = lax.top_k(logits, TOP_K)
    gate = jax.nn.softmax(top_val, axis=-1)
    n_assign = n_tok * TOP_K
    expert_flat = top_idx.reshape(-1).astype(jnp.int32)
    token_flat = jnp.arange(n_assign, dtype=jnp.int32) // TOP_K
    gate_flat = gate.reshape(-1)
    order = jnp.argsort(expert_flat)
    sorted_expert = expert_flat[order]
    counts = jnp.zeros((N_EXPERTS,), jnp.int32).at[expert_flat].add(1)
    padded_counts = (counts + MOE_BLOCK - 1) // MOE_BLOCK * MOE_BLOCK
    start = jnp.cumsum(counts) - counts
    padded_end = jnp.cumsum(padded_counts)
    padded_start = padded_end - padded_counts
    dest = padded_start[sorted_expert] + jnp.arange(n_assign, dtype=jnp.int32) - start[sorted_expert]
    n_blocks = -(-n_assign // MOE_BLOCK) + N_EXPERTS
    n_slots = n_blocks * MOE_BLOCK
    slot_token = jnp.full((n_slots,), n_tok, jnp.int32).at[dest].set(token_flat[order])
    slot_gate = jnp.zeros((n_slots,), F32).at[dest].set(gate_flat[order])
    block_start = jnp.arange(n_blocks, dtype=jnp.int32) * MOE_BLOCK
    block_expert = jnp.minimum(jnp.searchsorted(padded_end, block_start, side='right'), N_EXPERTS - 1)
    h_pad = jnp.concatenate([hf, jnp.zeros((1, d), hf.dtype)], axis=0)
    w_gu, b_gu = p['mo_w_gu'][l], p['mo_b_gu'][l]
    w_down, b_down = p['mo_w_down'][l], p['mo_b_down'][l]

    def run_block(args):
        tok, e = args
        hb = h_pad[tok]
        gu = hb @ w_gu[e] + b_gu[e]
        g = jnp.minimum(gu[:, :D_FF], SWIGLU_LIMIT)
        lin = jnp.clip(gu[:, D_FF:], -SWIGLU_LIMIT, SWIGLU_LIMIT)
        act = g * jax.nn.sigmoid(SWIGLU_ALPHA * g) * (lin + 1.0)
        return act @ w_down[e] + b_down[e]

    out = lax.map(run_block, (slot_token.reshape(n_blocks, MOE_BLOCK), block_expert))
    out = out.reshape(n_slots, d) * slot_gate[:, None].astype(out.dtype)
    y = jnp.zeros((n_tok + 1, d), out.dtype).at[slot_token].add(out)[:n_tok]
    return y.reshape(bsz, L, d).astype(h.dtype)


def _encode(x, p):
    h = _layer_norm(x, p['ln0_g'], p['ln0_b'])
    for l in range(DEPTH):
        h = _layer_norm(DN_ALPHA * h + _mixer(h, l, p), p['ln1_g'][l], p['ln1_b'][l])
        h = _layer_norm(DN_ALPHA * h + _moe(h, l, p), p['ln2_g'][l], p['ln2_b'][l])
    return h


def setup_inputs(seed: int = 0) -> dict:
    key = jax.random.key(seed)
    ks = iter(jax.random.split(key, 48))

    def nrm(shape, scale):
        return jax.random.normal(next(ks), shape, F32) * scale

    G, P, H = SSM_GROUPS, SSM_STATE, SSM_GROUP_CH
    E, F = N_EXPERTS, D_FF
    a_im_base = jnp.pi * jnp.arange(P, dtype=F32)
    decay_base = jnp.abs(jnp.linspace(math.log(HY_TARGET) / HY_FAST_PCT,
                                      math.log(HY_TARGET) / HY_SLOW_PCT, HY_WIDTH, dtype=F32))
    return {
        'x_prompt': nrm((BATCH, SEQ, D_MODEL), 1.0),
        'x_sample': nrm((DEC_BATCH, DEC_SEQ, D_MODEL), 1.0),
        'ln0_g': 1.0 + nrm((D_MODEL,), 0.02),
        'ln0_b': nrm((D_MODEL,), 0.02),
        'w_in': nrm((DEPTH, D_MODEL, IN_WIDTH), D_MODEL ** -0.5),
        'ssm_A_re': -0.5 + nrm((DEPTH, 2, G, P), 0.01),
        'ssm_A_im': a_im_base + nrm((DEPTH, 2, G, P), 0.01),
        'ssm_log_step': jax.random.uniform(next(ks), (DEPTH, 2, G), F32,
                                           math.log(SSM_DT_MIN), math.log(SSM_DT_MAX)),
        'ssm_B_re': nrm((DEPTH, 2, G, P, H), (2 * H) ** -0.5),
        'ssm_B_im': nrm((DEPTH, 2, G, P, H), (2 * H) ** -0.5),
        'ssm_C_re': nrm((DEPTH, 2, G, H, P), (2 * P) ** -0.5),
        'ssm_C_im': nrm((DEPTH, 2, G, H, P), (2 * P) ** -0.5),
        'ssm_D': nrm((DEPTH, SSM_WIDTH), 1.0),
        'ssm_w_glu': nrm((DEPTH, SSM_WIDTH, SSM_WIDTH), SSM_WIDTH ** -0.5),
        'ssm_b_glu': nrm((DEPTH, SSM_WIDTH), 0.02),
        'hy_short_w': nrm((DEPTH, HY_SHORT, (HY_ORDER + 1) * HY_WIDTH), HY_SHORT ** -0.5),
        'hy_short_b': nrm((DEPTH, (HY_ORDER + 1) * HY_WIDTH), 0.02),
        'hf_w1': nrm((DEPTH, HY_EMB, HY_FILTER_HIDDEN), HY_EMB ** -0.5),
        'hf_b1': nrm((DEPTH, HY_FILTER_HIDDEN), 0.1),
        'hf_w2': nrm((DEPTH, HY_FILTER_HIDDEN, HY_FILTER_HIDDEN), HY_FILTER_HIDDEN ** -0.5),
        'hf_b2': nrm((DEPTH, HY_FILTER_HIDDEN), 0.1),
        'hf_w3': nrm((DEPTH, HY_FILTER_HIDDEN, HY_FILTER_HIDDEN), HY_FILTER_HIDDEN ** -0.5),
        'hf_b3': nrm((DEPTH, HY_FILTER_HIDDEN), 0.1),
        'hf_w4': nrm((DEPTH, HY_FILTER_HIDDEN, HY_ORDER * 2 * HY_WIDTH), 0.02 * HY_FILTER_HIDDEN ** -0.5),
        'hf_freq': 1.0 + nrm((DEPTH, HY_FILTER_HIDDEN), 0.01),
        'hf_decay': decay_base * (1.0 + nrm((DEPTH, HY_ORDER, 2, HY_WIDTH), 0.01)),
        'hy_filter_bias': nrm((DEPTH, HY_ORDER, HY_WIDTH), 1.0),
        'w_branch_a': nrm((DEPTH, SSM_WIDTH, D_MODEL), SSM_WIDTH ** -0.5),
        'w_branch_b': nrm((DEPTH, HY_WIDTH, D_MODEL), HY_WIDTH ** -0.5),
        'w_out': nrm((DEPTH, D_MODEL, D_MODEL), DN_BETA * D_MODEL ** -0.5),
        'ln1_g': 1.0 + nrm((DEPTH, D_MODEL), 0.02),
        'ln1_b': nrm((DEPTH, D_MODEL), 0.02),
        'mo_w_router': nrm((DEPTH, D_MODEL, E), D_MODEL ** -0.5),
        'mo_b_router': nrm((DEPTH, E), 0.01),
        'mo_w_gu': nrm((DEPTH, E, D_MODEL, 2 * F), D_MODEL ** -0.5),
        'mo_b_gu': nrm((DEPTH, E, 2 * F), 0.01),
        'mo_w_down': nrm((DEPTH, E, F, D_MODEL), DN_BETA * F ** -0.5),
        'mo_b_down': nrm((DEPTH, E, D_MODEL), 0.01),
        'ln2_g': 1.0 + nrm((DEPTH, D_MODEL), 0.02),
        'ln2_b': nrm((DEPTH, D_MODEL), 0.02),
    }


def reference(x_prompt, x_sample, ln0_g, ln0_b, w_in, ssm_A_re, ssm_A_im, ssm_log_step,
              ssm_B_re, ssm_B_im, ssm_C_re, ssm_C_im, ssm_D, ssm_w_glu, ssm_b_glu,
              hy_short_w, hy_short_b, hf_w1, hf_b1, hf_w2, hf_b2, hf_w3, hf_b3, hf_w4,
              hf_freq, hf_decay, hy_filter_bias, w_branch_a, w_branch_b, w_out,
              ln1_g, ln1_b, mo_w_router, mo_b_router, mo_w_gu, mo_b_gu, mo_w_down, mo_b_down,
              ln2_g, ln2_b):
    p = dict(ln0_g=ln0_g, ln0_b=ln0_b, w_in=w_in, ssm_A_re=ssm_A_re, ssm_A_im=ssm_A_im,
             ssm_log_step=ssm_log_step, ssm_B_re=ssm_B_re, ssm_B_im=ssm_B_im,
             ssm_C_re=ssm_C_re, ssm_C_im=ssm_C_im, ssm_D=ssm_D, ssm_w_glu=ssm_w_glu,
             ssm_b_glu=ssm_b_glu, hy_short_w=hy_short_w, hy_short_b=hy_short_b,
             hf_w1=hf_w1, hf_b1=hf_b1, hf_w2=hf_w2, hf_b2=hf_b2, hf_w3=hf_w3, hf_b3=hf_b3,
             hf_w4=hf_w4, hf_freq=hf_freq, hf_decay=hf_decay, hy_filter_bias=hy_filter_bias,
             w_branch_a=w_branch_a, w_branch_b=w_branch_b, w_out=w_out,
             ln1_g=ln1_g, ln1_b=ln1_b, mo_w_router=mo_w_router, mo_b_router=mo_b_router,
             mo_w_gu=mo_w_gu, mo_b_gu=mo_b_gu, mo_w_down=mo_w_down, mo_b_down=mo_b_down,
             ln2_g=ln2_g, ln2_b=ln2_b)
    y_prompt = _encode(x_prompt, p)
    y_sample = _encode(x_sample, p)
    return (y_prompt, y_sample)
```

```python
import functools
import math

import numpy as np
import jax
import jax.numpy as jnp
from jax import lax
from jax.experimental import pallas as pl
from jax.experimental.pallas import tpu as pltpu

F32 = jnp.float32
BF16 = jnp.bfloat16
HIGHEST = lax.Precision.HIGHEST

TOP_K = 4
LN_EPS = 1e-5
SWIGLU_ALPHA = 1.702
SWIGLU_LIMIT = 7.0
HY_SHIFT = 0.05

LANES = 128
SUBLANES = 8
VMEM_LIMIT_BYTES = 56 * 1024 * 1024

S5_CHUNK = 32
NEG_BIG = -1e30


def _tile(n, pref, align):
    if n <= pref:
        return n
    t = (pref // align) * align
    while t >= align:
        if n % t == 0:
            return t
        t -= align
    return n


def _params(sem):
    return pltpu.CompilerParams(dimension_semantics=sem, vmem_limit_bytes=VMEM_LIMIT_BYTES)


def _mm_kernel(*refs, nk, n_extra, n_out, epilogue):
    a_ref, b_ref = refs[:2]
    extra = refs[2:2 + n_extra]
    outs = refs[2 + n_extra:2 + n_extra + n_out]
    acc = refs[-1]
    k = pl.program_id(3)

    @pl.when(k == 0)
    def _init():
        acc[...] = jnp.zeros_like(acc)

    acc[...] += jnp.dot(a_ref[...], b_ref[...], preferred_element_type=F32)

    @pl.when(k == nk - 1)
    def _finish():
        res = epilogue(acc[...], *[e[...] for e in extra])
        if n_out == 1:
            res = (res,)
        for o, r in zip(outs, res):
            o[...] = r.astype(o.dtype)


def _mm(a, b, *, out_dtypes=(F32,), epilogue=None, extras=(), tm=1024, tn=1024, tk=1024, name="mm"):
    if epilogue is None:
        epilogue = lambda acc: acc
    g_sizes = [x.shape[0] for x in (a, b) if x.ndim == 3]
    g_sizes += [x.shape[0] for x, kind in extras if kind == "mn" and x.ndim == 3]
    G = g_sizes[0] if g_sizes else 1
    M, K = a.shape[-2:]
    N = b.shape[-1]
    tm = _tile(M, tm, 16)
    tn = _tile(N, tn, LANES)
    tk = _tile(K, tk, LANES)
    nk = K // tk

    def spec(x, blk, imap):
        if x.ndim == 3:
            return pl.BlockSpec((None,) + blk, lambda g, i, j, k: (g,) + imap(i, j, k))
        return pl.BlockSpec(blk, lambda g, i, j, k: imap(i, j, k))

    in_specs = [spec(a, (tm, tk), lambda i, j, k: (i, k)),
                spec(b, (tk, tn), lambda i, j, k: (k, j))]
    args = [a, b]
    for x, kind in extras:
        if kind == "mn":
            in_specs.append(spec(x, (tm, tn), lambda i, j, k: (i, j)))
        else:
            in_specs.append(pl.BlockSpec((1, tn), lambda g, i, j, k: (0, j)))
        args.append(x)
    batched = bool(g_sizes)
    out_shape, out_specs = [], []
    for dt in out_dtypes:
        if batched:
            out_shape.append(jax.ShapeDtypeStruct((G, M, N), dt))
            out_specs.append(pl.BlockSpec((None, tm, tn), lambda g, i, j, k: (g, i, j)))
        else:
            out_shape.append(jax.ShapeDtypeStruct((M, N), dt))
            out_specs.append(pl.BlockSpec((tm, tn), lambda g, i, j, k: (i, j)))
    res = pl.pallas_call(
        functools.partial(_mm_kernel, nk=nk, n_extra=len(extras), n_out=len(out_dtypes),
                          epilogue=epilogue),
        grid=(G, M // tm, N // tn, nk),
        in_specs=in_specs,
        out_specs=out_specs,
        out_shape=out_shape,
        scratch_shapes=[pltpu.VMEM((tm, tn), F32)],
        compiler_params=_params(("parallel", "parallel", "parallel", "arbitrary")),
        name=name,
    )(*args)
    return res[0] if len(out_dtypes) == 1 else res


def _ln_rows(x, g, b):
    mu = jnp.mean(x, axis=-1, keepdims=True)
    xc = x - mu
    var = jnp.mean(xc * xc, axis=-1, keepdims=True)
    return xc * lax.rsqrt(var + LN_EPS) * g + b


def _ln_kernel(*refs, alpha, has_res):
    if has_res:
        x_ref, r_ref, g_ref, b_ref, o_ref, ob_ref = refs
        x = alpha * r_ref[...] + x_ref[...]
    else:
        x_ref, g_ref, b_ref, o_ref, ob_ref = refs
        x = x_ref[...]
    y = _ln_rows(x, g_ref[...], b_ref[...])
    o_ref[...] = y
    ob_ref[...] = y.astype(BF16)


def _layer_norm(x, g, b, res=None, alpha=1.0):
    n, d = x.shape
    tr = _tile(n, 256, 16)
    row = pl.BlockSpec((tr, d), lambda i: (i, 0))
    vec = pl.BlockSpec((1, d), lambda i: (0, 0))
    args = [x] + ([res] if res is not None else []) + [g.reshape(1, d), b.reshape(1, d)]
    in_specs = [row] + ([row] if res is not None else []) + [vec, vec]
    return pl.pallas_call(
        functools.partial(_ln_kernel, alpha=alpha, has_res=res is not None),
        grid=(n // tr,),
        in_specs=in_specs,
        out_specs=[row, row],
        out_shape=[jax.ShapeDtypeStruct((n, d), F32), jax.ShapeDtypeStruct((n, d), BF16)],
        compiler_params=_params(("parallel",)),
        name="layer_norm",
    )(*args)


def _shortconv_kernel(u_ref, prev_ref, next_ref, w_ref, b_ref, o_ref, buf, *, tl, n_prompt, l_prompt, l_sample):
    i = pl.program_id(0)
    r0 = i * tl
    in_prompt = r0 < n_prompt
    seq_len = jnp.where(in_prompt, l_prompt, l_sample)
    off = jnp.where(in_prompt, r0, r0 - n_prompt)
    pos = lax.rem(off, seq_len)
    has_prev = (pos != 0).astype(F32)
    has_next = (pos + tl != seq_len).astype(F32)
    buf[0:SUBLANES, :] = prev_ref[...] * has_prev
    buf[SUBLANES:SUBLANES + tl, :] = u_ref[...]
    buf[SUBLANES + tl:, :] = next_ref[...] * has_next
    w = w_ref[...]
    o_ref[...] = (buf[SUBLANES - 1:SUBLANES - 1 + tl, :] * w[0:1, :]
                  + buf[SUBLANES:SUBLANES + tl, :] * w[1:2, :]
                  + buf[SUBLANES + 1:SUBLANES + 1 + tl, :] * w[2:3, :]
                  + b_ref[...])


def _shortconv(u, w, b, part, width, n_prompt, l_prompt, l_sample):
    n = u.shape[0]
    tl = _tile(math.gcd(l_prompt, l_sample), 512, SUBLANES)
    tc = _tile(width, 1024, LANES)
    ncb = width // tc
    tb = tl // SUBLANES
    nrb = n // SUBLANES
    return pl.pallas_call(
        functools.partial(_shortconv_kernel, tl=tl, n_prompt=n_prompt, l_prompt=l_prompt, l_sample=l_sample),
        grid=(n // tl, ncb),
        in_specs=[
            pl.BlockSpec((tl, tc), lambda i, j: (i, part * ncb + j)),
            pl.BlockSpec((SUBLANES, tc), lambda i, j: (jnp.maximum(i * tb - 1, 0), part * ncb + j)),
            pl.BlockSpec((SUBLANES, tc), lambda i, j: (jnp.minimum((i + 1) * tb, nrb - 1), part * ncb + j)),
            pl.BlockSpec((3, tc), lambda i, j: (0, part * ncb + j)),
            pl.BlockSpec((1, tc), lambda i, j: (0, part * ncb + j)),
        ],
        out_specs=pl.BlockSpec((tl, tc), lambda i, j: (i, j)),
        out_shape=jax.ShapeDtypeStruct((n, width), F32),
        scratch_shapes=[pltpu.VMEM((tl + 2 * SUBLANES, tc), F32)],
        compiler_params=_params(("parallel", "parallel")),
        name="hyena_shortconv",
    )(u, u, u, w, b.reshape(1, -1))


def _s5_in_kernel(u_ref, wm_ref, ws_ref, yi_ref, z_ref, *, thw):
    u = u_ref[...]
    for gi in range(2):
        yi_ref[:, gi * thw:(gi + 1) * thw] = jnp.dot(
            u[:, gi * thw:(gi + 1) * thw], wm_ref[gi], preferred_element_type=F32)
    z_ref[...] = jnp.dot(u, ws_ref[0], preferred_element_type=F32)


def _s5_in(u2, wm, ws, thw):
    r = u2.shape[0]
    npair = wm.shape[0] // 2
    sw = ws.shape[-1]
    tr = _tile(r, 1024, 16)
    return pl.pallas_call(
        functools.partial(_s5_in_kernel, thw=thw),
        grid=(npair, r // tr),
        in_specs=[
            pl.BlockSpec((tr, 2 * thw), lambda p, i: (i, p)),
            pl.BlockSpec((2, thw, thw), lambda p, i: (p, 0, 0)),
            pl.BlockSpec((1, 2 * thw, sw), lambda p, i: (p, 0, 0)),
        ],
        out_specs=[
            pl.BlockSpec((tr, 2 * thw), lambda p, i: (i, p)),
            pl.BlockSpec((tr, sw), lambda p, i: (i, p)),
        ],
        out_shape=[jax.ShapeDtypeStruct((r, npair * 2 * thw), F32),
                   jax.ShapeDtypeStruct((r, npair * sw), F32)],
        compiler_params=_params(("parallel", "parallel")),
        name="s5_chunk_in",
    )(u2, wm, ws)


def _s5_scan_kernel(z_ref, arf_ref, aif_ref, arb_ref, aib_ref, s_ref, *, nchunk, npb, nseg_prompt, ratio):
    q = LANES
    rows = lax.broadcasted_iota(jnp.int32, (SUBLANES, q), 0)
    has_pred = rows < 0
    has_succ = rows < 0
    for r in range(nseg_prompt, SUBLANES):
        if (r - nseg_prompt) % ratio != 0:
            has_pred = has_pred | (rows == r)
        if (r - nseg_prompt) % ratio != ratio - 1:
            has_succ = has_succ | (rows == r)

    def coef(p):
        sl = slice(p * q, (p + 1) * q)
        return (arf_ref[:, sl], aif_ref[:, sl], arb_ref[:, sl], aib_ref[:, sl])

    def lanes(p, k):
        return slice(p * 4 * q + k * q, p * 4 * q + (k + 1) * q)

    def cmul(ar, ai, sr, si):
        return ar * sr - ai * si, ar * si + ai * sr

    zero = jnp.zeros((SUBLANES, q), F32)
    coefs = [coef(p) for p in range(npb)]

    def main_body(c, carry):
        cb = nchunk - 1 - c
        new = []
        for p in range(npb):
            fr, fi, br, bi = carry[4 * p:4 * p + 4]
            arf, aif, arb, aib = coefs[p]
            s_ref[c, :, lanes(p, 0)] = fr
            s_ref[c, :, lanes(p, 1)] = fi
            s_ref[cb, :, lanes(p, 2)] = br
            s_ref[cb, :, lanes(p, 3)] = bi
            nfr, nfi = cmul(arf, aif, fr, fi)
            nbr, nbi = cmul(arb, aib, br, bi)
            new += [nfr + z_ref[c, :, lanes(p, 0)], nfi + z_ref[c, :, lanes(p, 1)],
                    nbr + z_ref[cb, :, lanes(p, 2)], nbi + z_ref[cb, :, lanes(p, 3)]]
        return tuple(new)

    fin = lax.fori_loop(0, nchunk, main_body, tuple([zero] * (4 * npb)))

    def fix_body(c, carry):
        cb = nchunk - 1 - c
        new = []
        for p in range(npb):
            fr, fi, br, bi = carry[4 * p:4 * p + 4]
            arf, aif, arb, aib = coefs[p]
            s_ref[c, :, lanes(p, 0)] += fr
            s_ref[c, :, lanes(p, 1)] += fi
            s_ref[cb, :, lanes(p, 2)] += br
            s_ref[cb, :, lanes(p, 3)] += bi
            new += list(cmul(arf, aif, fr, fi)) + list(cmul(arb, aib, br, bi))
        return tuple(new)

    for _ in range(ratio - 1):
        init = []
        for p in range(npb):
            fr, fi, br, bi = fin[4 * p:4 * p + 4]
            init += [jnp.where(has_pred, pltpu.roll(fr, 1, 0), 0.0),
                     jnp.where(has_pred, pltpu.roll(fi, 1, 0), 0.0),
                     jnp.where(has_succ, pltpu.roll(br, SUBLANES - 1, 0), 0.0),
                     jnp.where(has_succ, pltpu.roll(bi, SUBLANES - 1, 0), 0.0)]
        fin = lax.fori_loop(0, nchunk, fix_body, tuple(init))


def _s5_scan(z3, arf, aif, arb, aib, nseg_prompt, ratio):
    nchunk, nseg, cols = z3.shape
    assert nseg == SUBLANES
    sw = 4 * LANES
    npb = 2 if (cols // sw) % 2 == 0 else 1
    cb = npb * sw
    coef_spec = pl.BlockSpec((1, npb * LANES), lambda j: (0, j))
    blk = pl.BlockSpec((nchunk, nseg, cb), lambda j: (0, 0, j))
    return pl.pallas_call(
        functools.partial(_s5_scan_kernel, nchunk=nchunk, npb=npb, nseg_prompt=nseg_prompt, ratio=ratio),
        grid=(cols // cb,),
        in_specs=[blk, coef_spec, coef_spec, coef_spec, coef_spec],
        out_specs=blk,
        out_shape=jax.ShapeDtypeStruct(z3.shape, F32),
        compiler_params=_params(("parallel",)),
        name="s5_chunk_scan",
    )(z3, arf, aif, arb, aib)


def _gelu_tanh(x):
    return 0.5 * x * (1.0 + jnp.tanh(math.sqrt(2.0 / math.pi) * (x + 0.044715 * (x * x * x))))


def _s5_out_kernel(yi_ref, s_ref, cc_ref, o_ref):
    y = yi_ref[...] + jnp.dot(s_ref[...].astype(BF16), cc_ref[0], preferred_element_type=F32)
    o_ref[...] = _gelu_tanh(y).astype(o_ref.dtype)


def _s5_out(yi, s2, cc, thw):
    r = yi.shape[0]
    npair = cc.shape[0]
    sw = cc.shape[1]
    tr = _tile(r, 1024, 16)
    return pl.pallas_call(
        _s5_out_kernel,
        grid=(npair, r // tr),
        in_specs=[
            pl.BlockSpec((tr, 2 * thw), lambda p, i: (i, p)),
            pl.BlockSpec((tr, sw), lambda p, i: (i, p)),
            pl.BlockSpec((1, sw, 2 * thw), lambda p, i: (p, 0, 0)),
        ],
        out_specs=pl.BlockSpec((tr, 2 * thw), lambda p, i: (i, p)),
        out_shape=jax.ShapeDtypeStruct(yi.shape, BF16),
        compiler_params=_params(("parallel", "parallel")),
        name="s5_chunk_out",
    )(yi, s2, cc)


def _s5_weights(p, l, t):
    a_re, a_im = p["ssm_A_re"][l].astype(F32), p["ssm_A_im"][l].astype(F32)
    ndir, g, ns = a_re.shape
    h = p["ssm_B_re"].shape[-1]
    lam = lax.complex(a_re, a_im)
    step = jnp.exp(p["ssm_log_step"][l].astype(F32))[..., None]
    lam_bar = jnp.exp(lam * step)
    b_bar = ((lam_bar - 1.0) / lam)[..., None] * lax.complex(p["ssm_B_re"][l].astype(F32), p["ssm_B_im"][l].astype(F32))
    c_mat = lax.complex(p["ssm_C_re"][l].astype(F32), p["ssm_C_im"][l].astype(F32))
    jj = jnp.arange(t + 1, dtype=F32)
    pw = jnp.exp((lam * step)[..., None] * jj)
    kf = jnp.real(jnp.einsum("gop,gpd,gpk->gdok", c_mat[0], pw[0, :, :, :t], b_bar[0], precision=HIGHEST))
    kb = jnp.real(jnp.einsum("gop,gpd,gpk->gdok", c_mat[1], pw[1, :, :, :t], b_bar[1], precision=HIGHEST))
    ii = np.arange(t)
    ef = (ii[None, None, :] - ii[None, :, None] == ii[:, None, None]).astype(np.float32)
    eb = (ii[None, :, None] - ii[None, None, :] == ii[:, None, None]).astype(np.float32)
    m = (jnp.einsum("dji,gdok->gjkio", ef, kf, precision=HIGHEST)
         + jnp.einsum("dji,gdok->gjkio", eb, kb, precision=HIGHEST))
    d_skip = p["ssm_D"][l].astype(F32).reshape(g, h)
    eye_t = np.eye(t, dtype=np.float32)
    eye_h = np.eye(h, dtype=np.float32)
    m = m + jnp.einsum("ji,ko,go->gjkio", eye_t, eye_h, d_skip)
    thw = t * h
    wm = m.reshape(g, thw, thw)
    bf = pw[0, :, :, :t][..., ::-1][:, None, :, :] * jnp.transpose(b_bar[0], (0, 2, 1))[..., None]
    bb = pw[1, :, :, :t][:, None, :, :] * jnp.transpose(b_bar[1], (0, 2, 1))[..., None]
    wsg = jnp.stack([jnp.real(bf), jnp.imag(bf), jnp.real(bb), jnp.imag(bb)], axis=0)
    wsg = jnp.transpose(wsg, (1, 4, 2, 0, 3))
    wsg = wsg.reshape(g // 2, 2, thw, 4, ns)
    eye2 = np.eye(2, dtype=np.float32)
    ws = jnp.einsum("narqp,ab->narqbp", wsg, eye2).reshape(g // 2, 2 * thw, 4 * 2 * ns)
    cf = c_mat[0][..., None] * pw[0, :, None, :, 1:t + 1]
    cb = c_mat[1][..., None] * pw[1, :, None, :, 1:t + 1][..., ::-1]
    ccg = jnp.stack([jnp.real(cf), -jnp.imag(cf), jnp.real(cb), -jnp.imag(cb)], axis=0)
    ccg = jnp.transpose(ccg, (1, 0, 3, 4, 2))
    ccg = ccg.reshape(g // 2, 2, 4, ns, thw)
    cc = jnp.einsum("naqpc,ab->nqbpac", ccg, eye2).reshape(g // 2, 4 * 2 * ns, 2 * thw)
    a_t = pw[:, :, :, t]
    coefs = [jnp.real(a_t[0]).reshape(1, -1), jnp.imag(a_t[0]).reshape(1, -1),
             jnp.real(a_t[1]).reshape(1, -1), jnp.imag(a_t[1]).reshape(1, -1)]
    return wm.astype(BF16), ws.astype(BF16), cc.astype(BF16), coefs


def _s5_branch(u_a, p, l, nseg, lseg, nseg_prompt, ratio):
    ntok, width = u_a.shape
    h = p["ssm_B_re"].shape[-1]
    g = width // h
    t = S5_CHUNK
    thw = t * h
    nchunk = lseg // t
    wm, ws, cc, coefs = _s5_weights(p, l, t)
    u2 = u_a.reshape(nseg, nchunk, t, g, h).transpose(1, 0, 3, 2, 4).reshape(nchunk * nseg, g * thw)
    yi, z = _s5_in(u2, wm, ws, thw)
    s3 = _s5_scan(z.reshape(nchunk, nseg, -1), *coefs, nseg_prompt, ratio)
    y2 = _s5_out(yi, s3.reshape(nchunk * nseg, -1), cc, thw)
    return y2.reshape(nchunk, nseg, g, t, h).transpose(1, 0, 3, 2, 4).reshape(ntok, width)


def _filter_kernel(z_ref, w1_ref, b1_ref, w2_ref, b2_ref, w3_ref, b3_ref, w4_ref, fr_ref, dec_ref, o_ref):
    z = z_ref[...]
    fr = fr_ref[...]
    hcur = jnp.sin(fr * (jnp.dot(z, w1_ref[...], precision=HIGHEST, preferred_element_type=F32) + b1_ref[...]))
    hcur = jnp.sin(fr * (jnp.dot(hcur, w2_ref[...], precision=HIGHEST, preferred_element_type=F32) + b2_ref[...]))
    hcur = jnp.sin(fr * (jnp.dot(hcur, w3_ref[...], precision=HIGHEST, preferred_element_type=F32) + b3_ref[...]))
    k = jnp.dot(hcur, w4_ref[...], precision=HIGHEST, preferred_element_type=F32)
    t = z[:, 0:1]
    o_ref[...] = k * (jnp.exp(-t * jnp.abs(dec_ref[...])) + HY_SHIFT)


def _implicit_filters(length, p, l):
    emb = p["hf_w1"].shape[1]
    hid = p["hf_w1"].shape[2]
    bands = (emb - 1) // 2
    tt = jnp.linspace(0.0, 1.0, length, dtype=F32)[:, None]
    w = (2.0 * math.pi / length) * jnp.arange(length, dtype=F32)[:, None]
    f = jnp.linspace(1e-4, bands - 1, bands, dtype=F32)[None, :]
    z = jnp.concatenate([tt, jnp.cos(f * w), -jnp.sin(f * w)], axis=-1)
    zp = jnp.pad(z, ((0, 0), (0, LANES - emb)))
    w1 = jnp.pad(p["hf_w1"][l].astype(F32), ((0, LANES - emb), (0, 0)))
    cols = p["hf_w4"].shape[-1]
    tl = _tile(length, 256, SUBLANES)
    full = lambda shape: pl.BlockSpec(shape, lambda i: (0,) * len(shape))
    return pl.pallas_call(
        _filter_kernel,
        grid=(length // tl,),
        in_specs=[pl.BlockSpec((tl, LANES), lambda i: (i, 0)),
                  full((LANES, hid)), full((1, hid)), full((hid, hid)), full((1, hid)),
                  full((hid, hid)), full((1, hid)), full((hid, cols)), full((1, hid)), full((1, cols))],
        out_specs=pl.BlockSpec((tl, cols), lambda i: (i, 0)),
        out_shape=jax.ShapeDtypeStruct((length, cols), F32),
        compiler_params=_params(("parallel",)),
        name="hyena_filter_mlp",
    )(zp, w1, p["hf_b1"][l].astype(F32).reshape(1, hid), p["hf_w2"][l].astype(F32),
      p["hf_b2"][l].astype(F32).reshape(1, hid), p["hf_w3"][l].astype(F32),
      p["hf_b3"][l].astype(F32).reshape(1, hid), p["hf_w4"][l].astype(F32),
      p["hf_freq"][l].astype(F32).reshape(1, hid), p["hf_decay"][l].astype(F32).reshape(1, cols))


def _dft_tables(lseg, tq):
    n = 2 * lseg
    nfreq = lseg
    tb = 1
    while tb * tb < lseg:
        tb *= 2
    ta = lseg // tb
    k2 = 2 * jnp.arange(nfreq, dtype=jnp.int32)[:, None] + 1

    def cs(tvals):
        m = (k2 * tvals[None, :]) % (2 * n)
        ang = m.astype(F32) * (math.pi / n)
        return jnp.cos(ang), jnp.sin(ang)

    ca, sa = cs(jnp.arange(ta, dtype=jnp.int32) * tb)
    cb, sb = cs(jnp.arange(tb, dtype=jnp.int32))
    cos = (ca[:, :, None] * cb[:, None, :] - sa[:, :, None] * sb[:, None, :]).reshape(nfreq, lseg)
    nsin = -(sa[:, :, None] * cb[:, None, :] + ca[:, :, None] * sb[:, None, :]).reshape(nfreq, lseg)
    cos_b, nsin_b = cos.astype(BF16), nsin.astype(BF16)
    nq = nfreq // tq
    f_fwd = jnp.stack([cos_b.reshape(nq, tq, lseg), nsin_b.reshape(nq, tq, lseg)], axis=1).reshape(n, lseg)
    f_inv = (f_fwd.T.astype(F32) * (2.0 / n)).astype(BF16)
    return cos_b, nsin_b, f_fwd, f_inv


def _spec_mul_kernel(x_ref, kre_ref, kim_ref, o_ref, *, tq, nseg_prompt, ratio, nseq_sample):
    def cmul(xr, xi, kr, ki):
        return xr * kr - xi * ki, xr * ki + xi * kr

    def put(s, yr, yi):
        o_ref[s, 0:tq, :] = yr.astype(o_ref.dtype)
        o_ref[s, tq:2 * tq, :] = yi.astype(o_ref.dtype)

    for s in range(nseg_prompt):
        yr, yi = cmul(x_ref[s, 0:tq, :], x_ref[s, tq:2 * tq, :], kre_ref[0], kim_ref[0])
        put(s, yr, yi)
    for q in range(nseq_sample):
        base = nseg_prompt + q * ratio
        for a in range(ratio):
            acc_r = acc_i = None
            for b in range(ratio):
                kidx = 1 + (a - b) + (ratio - 1)
                yr, yi = cmul(x_ref[base + b, 0:tq, :], x_ref[base + b, tq:2 * tq, :], kre_ref[kidx], kim_ref[kidx])
                acc_r = yr if acc_r is None else acc_r + yr
                acc_i = yi if acc_i is None else acc_i + yi
            put(base + a, acc_r, acc_i)


def _spec_mul(xf, kre, kim, tq, nseg_prompt, ratio):
    nseg, n, w = xf.shape
    nspec = kre.shape[0]
    tc = _tile(w, 512, LANES)
    nseq_sample = (nseg - nseg_prompt) // ratio
    return pl.pallas_call(
        functools.partial(_spec_mul_kernel, tq=tq, nseg_prompt=nseg_prompt, ratio=ratio, nseq_sample=nseq_sample),
        grid=(n // (2 * tq), w // tc),
        in_specs=[pl.BlockSpec((nseg, 2 * tq, tc), lambda i, j: (0, i, j)),
                  pl.BlockSpec((nspec, tq, tc), lambda i, j: (0, i, j)),
                  pl.BlockSpec((nspec, tq, tc), lambda i, j: (0, i, j))],
        out_specs=pl.BlockSpec((nseg, 2 * tq, tc), lambda i, j: (0, i, j)),
        out_shape=jax.ShapeDtypeStruct(xf.shape, BF16),
        compiler_params=_params(("parallel", "parallel")),
        name="hyena_spectrum_mul",
    )(xf, kre, kim)


def _filter_sequences(kp, ks, lseg, ratio, w):
    order = kp.shape[1] // (2 * w)
    kp = kp.reshape(lseg, order, 2, w)
    ks = ks.reshape(ratio * lseg, order, 2, w)
    zero_row = jnp.zeros((1, order, w), F32)
    plus, minus = [], []

    def add(c, a):
        plus.append(c + a)
        minus.append(c - a)

    add(kp[:, :, 0], kp[:, :, 1])
    fwd = jnp.concatenate([ks[:, :, 0], jnp.zeros((lseg, order, w), F32)], axis=0)
    bwd = jnp.concatenate([ks[:, :, 1], jnp.zeros((lseg, order, w), F32)], axis=0)
    for d in range(-(ratio - 1), ratio):
        if d == 0:
            add(fwd[:lseg], bwd[:lseg])
        elif d > 0:
            c = fwd[d * lseg:(d + 1) * lseg]
            a = jnp.concatenate([zero_row, fwd[(d - 1) * lseg + 1:d * lseg][::-1]], axis=0)
            add(c, a)
        else:
            dd = -d
            c = bwd[(dd - 1) * lseg + 1:dd * lseg + 1][::-1]
            a = jnp.concatenate([zero_row, bwd[dd * lseg + 1:(dd + 1) * lseg]], axis=0)
            add(c, a)
    plus = jnp.stack(plus, axis=0).transpose(2, 0, 1, 3)
    minus = jnp.stack(minus, axis=0).transpose(2, 0, 1, 3)
    return plus, minus


def _hyena_branch(x1, x2, v, p, l, nseg, lseg, nseg_prompt, ratio):
    ntok, w = v.shape
    n = 2 * lseg
    tq = _tile(lseg, 256, 16)
    cos_b, nsin_b, f_fwd, f_inv = _dft_tables(lseg, tq)
    kp = _implicit_filters(lseg, p, l)
    ks = _implicit_filters(ratio * lseg, p, l) if ratio > 1 else kp
    plus, minus = _filter_sequences(kp, ks, lseg, ratio, w)
    order, nspec = plus.shape[:2]
    kre = _mm(cos_b, plus.reshape(order * nspec, lseg, w).astype(BF16), name="hyena_filter_dft_re")
    kim = _mm(nsin_b, minus.reshape(order * nspec, lseg, w).astype(BF16), name="hyena_filter_dft_im")
    kre = kre.reshape(order, nspec, lseg, w)
    kim = kim.reshape(order, nspec, lseg, w)
    fb = p["hy_filter_bias"][l].astype(F32)
    gates = (x1, x2)
    y = v
    for o in range(order):
        y3 = y.reshape(nseg, lseg, w)
        xf = _mm(f_fwd, y3.astype(BF16), name="hyena_dft_fwd")
        yf = _spec_mul(xf, kre[o], kim[o], tq, nseg_prompt, ratio)
        last = o == order - 1
        y = _mm(f_inv, yf,
                epilogue=lambda acc, gate, vin, bias: gate * (acc + vin * bias),
                extras=[(gates[o].reshape(nseg, lseg, w), "mn"), (y3, "mn"), (fb[o].reshape(1, w), "n")],
                out_dtypes=(BF16 if last else F32,), name="hyena_dft_inv").reshape(ntok, w)
    return y


def _router_kernel(h_ref, w_ref, b_ref, gate_ref, idx_ref, mask_ref):
    logits = jnp.dot(h_ref[...], w_ref[...], precision=HIGHEST, preferred_element_type=F32) + b_ref[...]
    lane = lax.broadcasted_iota(jnp.int32, logits.shape, 1).astype(F32)
    vals = logits
    tops, idxs = [], []
    mask = jnp.zeros(logits.shape, F32)
    for _ in range(TOP_K):
        m = jnp.max(vals, axis=-1, keepdims=True)
        idx = jnp.min(jnp.where(vals == m, lane, float(LANES)), axis=-1, keepdims=True)
        sel = lane == idx
        mask = jnp.where(sel, 1.0, mask)
        vals = jnp.where(sel, -jnp.inf, vals)
        tops.append(m)
        idxs.append(idx)
    es = [jnp.exp(t - tops[0]) for t in tops]
    denom = es[0]
    for e in es[1:]:
        denom = denom + e
    gate_c = jnp.zeros(logits.shape, F32)
    idx_c = jnp.zeros(logits.shape, F32)
    for k in range(TOP_K):
        gate_c = jnp.where(lane == k, es[k] / denom, gate_c)
        idx_c = jnp.where(lane == k, idxs[k], idx_c)
    gate_ref[...] = gate_c
    idx_ref[...] = idx_c.astype(jnp.int32)
    mask_ref[...] = mask.astype(BF16)


def _router(h, w_router, b_router):
    n, d = h.shape
    e = w_router.shape[-1]
    wp = jnp.pad(w_router.astype(F32), ((0, 0), (0, LANES - e)))
    bp = jnp.pad(b_router.astype(F32), (0, LANES - e), constant_values=NEG_BIG).reshape(1, LANES)
    tr = _tile(n, 256, 16)
    row = pl.BlockSpec((tr, LANES), lambda i: (i, 0))
    return pl.pallas_call(
        _router_kernel,
        grid=(n // tr,),
        in_specs=[pl.BlockSpec((tr, d), lambda i: (i, 0)),
                  pl.BlockSpec((d, LANES), lambda i: (0, 0)),
                  pl.BlockSpec((1, LANES), lambda i: (0, 0))],
        out_specs=[row, row, row],
        out_shape=[jax.ShapeDtypeStruct((n, LANES), F32), jax.ShapeDtypeStruct((n, LANES), jnp.int32),
                   jax.ShapeDtypeStruct((n, LANES), BF16)],
        compiler_params=_params(("parallel",)),
        name="moe_router",
    )(h, wp, bp)


def _rank_kernel(mask_ref, idx_ref, dest_ref, meta_ref, counts, carry, pstart, *, tr, block):
    ps = pl.program_id(0)
    i = pl.program_id(1)
    mask = mask_ref[...]
    colsum = jnp.sum(mask.astype(F32), axis=0, keepdims=True)

    @pl.when((ps == 0) & (i == 0))
    def _zero():
        counts[...] = jnp.zeros_like(counts)

    @pl.when(ps == 0)
    def _count():
        counts[...] += colsum
        dest_ref[...] = jnp.zeros_like(dest_ref)

    @pl.when((ps == 1) & (i == 0))
    def _starts():
        padded = jnp.floor((counts[...] + (block - 1)) / block) * block
        r = lax.broadcasted_iota(jnp.int32, (LANES, LANES), 0)
        c = lax.broadcasted_iota(jnp.int32, (LANES, LANES), 1)
        upper = (r < c).astype(F32)
        start = jnp.dot(jnp.broadcast_to(padded, (SUBLANES, LANES)), upper, precision=HIGHEST,
                        preferred_element_type=F32)[0:1, :]
        pstart[...] = start
        carry[...] = jnp.zeros_like(carry)
        rows = lax.broadcasted_iota(jnp.int32, (SUBLANES, LANES), 0)
        meta_ref[...] = jnp.where(rows == 0, start, jnp.where(rows == 1, start + padded, 0.0))

    @pl.when(ps == 1)
    def _rank():
        r = lax.broadcasted_iota(jnp.int32, (tr, tr), 0)
        c = lax.broadcasted_iota(jnp.int32, (tr, tr), 1)
        lower = (c < r).astype(BF16)
        rank = jnp.dot(lower, mask, preferred_element_type=F32)
        slot = pstart[...] + carry[...] + rank
        lane = lax.broadcasted_iota(jnp.int32, (tr, LANES), 1)
        idx_c = idx_ref[...]
        dest = jnp.zeros((tr, LANES), F32)
        for k in range(TOP_K):
            sel = lane == idx_c[:, k:k + 1]
            dk = jnp.sum(jnp.where(sel, slot, 0.0), axis=-1, keepdims=True)
            dest = jnp.where(lane == k, dk, dest)
        dest_ref[...] = dest.astype(jnp.int32)
        carry[...] += colsum


def _rank(mask, idx_c, block):
    n = mask.shape[0]
    tr = _tile(n, 256, 16)
    row = lambda: pl.BlockSpec((tr, LANES), lambda ps, i: (i, 0))
    return pl.pallas_call(
        functools.partial(_rank_kernel, tr=tr, block=block),
        grid=(2, n // tr),
        in_specs=[row(), row()],
        out_specs=[pl.BlockSpec((tr, LANES), lambda ps, i: (i * ps, 0)),
                   pl.BlockSpec((SUBLANES, LANES), lambda ps, i: (0, 0))],
        out_shape=[jax.ShapeDtypeStruct((n, LANES), jnp.int32), jax.ShapeDtypeStruct((SUBLANES, LANES), F32)],
        scratch_shapes=[pltpu.VMEM((1, LANES), F32), pltpu.VMEM((1, LANES), F32), pltpu.VMEM((1, LANES), F32)],
        compiler_params=_params(("arbitrary", "arbitrary")),
        name="moe_rank",
    )(mask, idx_c)


def _expert_kernel(be_ref, nbu_ref, tok_hbm, h3_hbm, wg_ref, wl_ref, bg_ref, bl_ref, wd_ref, bd_ref,
                   out_ref, x3, x2, idx, sem_idx, sem_rows, *, tm, nb, nslab, tdn):
    b = pl.program_id(0)
    f = pl.program_id(1)

    def load_idx(blk):
        cp = pltpu.make_async_copy(tok_hbm.at[blk], idx, sem_idx)
        cp.start()
        cp.wait()

    def start_rows():
        def body(t, carry):
            tok = idx[t // LANES, t % LANES]
            pltpu.make_async_copy(h3_hbm.at[tok], x3.at[t], sem_rows).start()
            return carry
        lax.fori_loop(0, tm, body, 0)

    def wait_rows():
        def body(t, carry):
            pltpu.make_async_copy(h3_hbm.at[0], x3.at[t], sem_rows).wait()
            return carry
        lax.fori_loop(0, tm, body, 0)

    @pl.when((f == 0) & (b == 0))
    def _prime():
        load_idx(0)
        start_rows()

    @pl.when(f == 0)
    def _stage():
        wait_rows()
        for k in range(nslab):
            x2[:, k * LANES:(k + 1) * LANES] = x3[:, k, :].astype(BF16)

        @pl.when(b + 1 < nb)
        def _next():
            load_idx(b + 1)
            start_rows()

    used = b < nbu_ref[0]

    def compute(first):
        x = x2[...]
        g = jnp.dot(x, wg_ref[0], preferred_element_type=F32) + bg_ref[0]
        lin = jnp.dot(x, wl_ref[0], preferred_element_type=F32) + bl_ref[0]
        g = jnp.minimum(g, SWIGLU_LIMIT)
        lin = jnp.clip(lin, -SWIGLU_LIMIT, SWIGLU_LIMIT)
        act = (g * jax.nn.sigmoid(SWIGLU_ALPHA * g) * (lin + 1.0)).astype(BF16)
        per = tdn // LANES
        for c in range(nslab // per):
            part = jnp.dot(act, wd_ref[0, :, c * tdn:(c + 1) * tdn], preferred_element_type=F32)
            if first:
                part = part + bd_ref[0, :, c * tdn:(c + 1) * tdn]
            for s in range(per):
                k = c * per + s
                if first:
                    out_ref[:, k, :] = part[:, s * LANES:(s + 1) * LANES]
                else:
                    out_ref[:, k, :] += part[:, s * LANES:(s + 1) * LANES]

    @pl.when(used & (f == 0))
    def _first():
        compute(True)

    @pl.when(used & (f != 0))
    def _rest():
        compute(False)

    @pl.when(jnp.logical_not(used) & (f == 0))
    def _unused():
        out_ref[...] = jnp.zeros_like(out_ref)


def _experts(h3, slot_token3, block_expert, nb_used, w_gu, b_gu, w_down, b_down, tm, tf):
    nb = slot_token3.shape[0]
    nslab = h3.shape[1]
    d = nslab * LANES
    e, _, f2 = w_gu.shape
    ff = f2 // 2
    nf = ff // tf
    tdn = _tile(d, 512, LANES)

    def fidx(b, f, nbu):
        return jnp.where(b < nbu[0], f, nf - 1)

    def eidx(b, be):
        return be[b]

    grid_spec = pltpu.PrefetchScalarGridSpec(
        num_scalar_prefetch=2,
        grid=(nb, nf),
        in_specs=[
            pl.BlockSpec(memory_space=pl.ANY),
            pl.BlockSpec(memory_space=pl.ANY),
            pl.BlockSpec((1, d, tf), lambda b, f, be, nbu: (eidx(b, be), 0, fidx(b, f, nbu))),
            pl.BlockSpec((1, d, tf), lambda b, f, be, nbu: (eidx(b, be), 0, nf + fidx(b, f, nbu))),
            pl.BlockSpec((1, 1, tf), lambda b, f, be, nbu: (eidx(b, be), 0, fidx(b, f, nbu))),
            pl.BlockSpec((1, 1, tf), lambda b, f, be, nbu: (eidx(b, be), 0, nf + fidx(b, f, nbu))),
            pl.BlockSpec((1, tf, d), lambda b, f, be, nbu: (eidx(b, be), fidx(b, f, nbu), 0)),
            pl.BlockSpec((1, 1, d), lambda b, f, be, nbu: (eidx(b, be), 0, 0)),
        ],
        out_specs=pl.BlockSpec((tm, nslab, LANES), lambda b, f, be, nbu: (b, 0, 0)),
        scratch_shapes=[
            pltpu.VMEM((tm, nslab, LANES), F32),
            pltpu.VMEM((tm, d), BF16),
            pltpu.SMEM(slot_token3.shape[1:], jnp.int32),
            pltpu.SemaphoreType.DMA,
            pltpu.SemaphoreType.DMA,
        ],
    )
    return pl.pallas_call(
        functools.partial(_expert_kernel, tm=tm, nb=nb, nslab=nslab, tdn=tdn),
        grid_spec=grid_spec,
        out_shape=jax.ShapeDtypeStruct((nb * tm, nslab, LANES), F32),
        compiler_params=_params(("arbitrary", "arbitrary")),
        name="moe_experts",
    )(block_expert, nb_used, slot_token3, h3, w_gu, w_gu, b_gu, b_gu, w_down, b_down)


def _combine_kernel(dest_hbm, o3_hbm, gate_ref, h_ref, g_ref, b_ref, y_ref, buf, pre, idx, sem_idx, sem_rows,
                    *, td, nslab, alpha):
    i = pl.program_id(0)
    cp = pltpu.make_async_copy(dest_hbm.at[i], idx, sem_idx)
    cp.start()
    cp.wait()

    def body(t, carry):
        for k in range(TOP_K):
            flat = t * TOP_K + k
            slot = idx[flat // LANES, flat % LANES]
            pltpu.make_async_copy(o3_hbm.at[slot], buf.at[k, t], sem_rows).start()
        return carry

    lax.fori_loop(0, td, body, 0)
    def wait_body(t, carry):
        for k in range(TOP_K):
            pltpu.make_async_copy(o3_hbm.at[0], buf.at[k, t], sem_rows).wait()
        return carry

    lax.fori_loop(0, td, wait_body, 0)
    gates = gate_ref[...]
    gk = [gates[:, k:k + 1] for k in range(TOP_K)]
    for c in range(nslab):
        moe = gk[0] * buf[0, :, c, :]
        for k in range(1, TOP_K):
            moe = moe + gk[k] * buf[k, :, c, :]
        pre[:, c * LANES:(c + 1) * LANES] = alpha * h_ref[:, c * LANES:(c + 1) * LANES] + moe
    y_ref[...] = _ln_rows(pre[...], g_ref[...], b_ref[...])


def _combine_ln(dest3, out3, gate_c, h, ln_g, ln_b, alpha):
    n, d = h.shape
    ntile = dest3.shape[0]
    td = n // ntile
    nslab = d // LANES
    return pl.pallas_call(
        functools.partial(_combine_kernel, td=td, nslab=nslab, alpha=alpha),
        grid=(ntile,),
        in_specs=[
            pl.BlockSpec(memory_space=pl.ANY),
            pl.BlockSpec(memory_space=pl.ANY),
            pl.BlockSpec((td, LANES), lambda i: (i, 0)),
            pl.BlockSpec((td, d), lambda i: (i, 0)),
            pl.BlockSpec((1, d), lambda i: (0, 0)),
            pl.BlockSpec((1, d), lambda i: (0, 0)),
        ],
        out_specs=pl.BlockSpec((td, d), lambda i: (i, 0)),
        out_shape=jax.ShapeDtypeStruct((n, d), F32),
        scratch_shapes=[
            pltpu.VMEM((TOP_K, td, nslab, LANES), F32),
            pltpu.VMEM((td, d), F32),
            pltpu.SMEM(dest3.shape[1:], jnp.int32),
            pltpu.SemaphoreType.DMA,
            pltpu.SemaphoreType.DMA,
        ],
        compiler_params=_params(("arbitrary",)),
        name="moe_combine_ln",
    )(dest3, out3, gate_c, h, ln_g.reshape(1, d), ln_b.reshape(1, d))


def _moe_ln(h, p, l, alpha, w_gu_b, w_down_b):
    n, d = h.shape
    e = p["mo_w_router"].shape[-1]
    tm = min(512, max(LANES, n * TOP_K // 8))
    td = _tile(n, 128, 32)
    ff = p["mo_w_down"].shape[2]
    tf = _tile(ff, 256, LANES)
    gate_c, idx_c, mask = _router(h, p["mo_w_router"][l], p["mo_b_router"][l])
    dest_c, meta = _rank(mask, idx_c, tm)
    nb = -(-(n * TOP_K) // tm) + e
    n_slots = nb * tm
    dest = dest_c[:, :TOP_K].reshape(-1)
    token_flat = jnp.arange(n * TOP_K, dtype=jnp.int32) // TOP_K
    slot_token = jnp.zeros((n_slots,), jnp.int32).at[dest].set(token_flat)
    pend = meta[1, :e].astype(jnp.int32)
    block_start = jnp.arange(nb, dtype=jnp.int32) * tm
    nb_used = pend[e - 1] // tm
    block_expert = jnp.minimum(jnp.searchsorted(pend, block_start, side="right"), e - 1).astype(jnp.int32)
    last_expert = block_expert[jnp.maximum(nb_used - 1, 0)]
    block_expert = jnp.where(block_start < pend[e - 1], block_expert, last_expert)
    h3 = h.reshape(n, d // LANES, LANES)
    out3 = _experts(h3, slot_token.reshape(nb, tm // LANES, LANES), block_expert, nb_used.reshape(1),
                    w_gu_b, p["mo_b_gu"][l].astype(F32).reshape(e, 1, -1),
                    w_down_b, p["mo_b_down"][l].astype(F32).reshape(e, 1, -1), tm, tf)
    dest3 = dest.reshape(n // td, td * TOP_K // LANES, LANES)
    return _combine_ln(dest3, out3, gate_c, h, p["ln2_g"][l].astype(F32), p["ln2_b"][l].astype(F32), alpha)


def _sigmoid_gate(acc, gate):
    return jax.nn.sigmoid(gate) * acc


def _sigmoid_gate_add(acc, prev, gate):
    return prev + jax.nn.sigmoid(gate) * acc


def _glu(acc, y, bias):
    return y.astype(F32) * jax.nn.sigmoid(acc + bias)


def kernel(x_prompt, x_sample, ln0_g, ln0_b, w_in, ssm_A_re, ssm_A_im, ssm_log_step, ssm_B_re, ssm_B_im, ssm_C_re, ssm_C_im, ssm_D, ssm_w_glu, ssm_b_glu, hy_short_w, hy_short_b, hf_w1, hf_b1, hf_w2, hf_b2, hf_w3, hf_b3, hf_w4, hf_freq, hf_decay, hy_filter_bias, w_branch_a, w_branch_b, w_out, ln1_g, ln1_b, mo_w_router, mo_b_router, mo_w_gu, mo_b_gu, mo_w_down, mo_b_down, ln2_g, ln2_b):
    p = dict(ssm_A_re=ssm_A_re, ssm_A_im=ssm_A_im, ssm_log_step=ssm_log_step, ssm_B_re=ssm_B_re,
             ssm_B_im=ssm_B_im, ssm_C_re=ssm_C_re, ssm_C_im=ssm_C_im, ssm_D=ssm_D,
             hf_w1=hf_w1, hf_b1=hf_b1, hf_w2=hf_w2, hf_b2=hf_b2, hf_w3=hf_w3, hf_b3=hf_b3, hf_w4=hf_w4,
             hf_freq=hf_freq, hf_decay=hf_decay, hy_filter_bias=hy_filter_bias,
             mo_w_router=mo_w_router, mo_b_router=mo_b_router, mo_b_gu=mo_b_gu, mo_w_down=mo_w_down,
             mo_b_down=mo_b_down, ln2_g=ln2_g, ln2_b=ln2_b)
    bp, lp, d = x_prompt.shape
    bs, ls, _ = x_sample.shape
    depth = w_in.shape[0]
    alpha = (2.0 * depth) ** 0.25
    lseg = min(lp, ls)
    assert lp == lseg and ls % lseg == 0, "sample sequences must be whole multiples of the prompt length"
    ratio = ls // lseg
    nseg_prompt = bp
    nseg = bp + bs * ratio
    assert nseg == SUBLANES, "the S5 scan keeps one segment per sublane"
    n_prompt = bp * lp
    ssm_w = ssm_D.shape[-1]
    hy_w = hy_filter_bias.shape[-1]
    c1 = ssm_w
    c2 = c1 + 3 * hy_w
    c3 = c2 + d

    x_all = jnp.concatenate([x_prompt.reshape(-1, d), x_sample.reshape(-1, d)], axis=0)
    h, hb = _layer_norm(x_all, ln0_g.astype(F32), ln0_b.astype(F32))
    for l in range(depth):
        w_in_b = w_in[l].astype(BF16)
        u_a = _mm(hb, w_in_b[:, :c1], out_dtypes=(BF16,), name="proj_s5")
        u_b = _mm(hb, w_in_b[:, c1:c2], name="proj_hyena")
        g_a = _mm(hb, w_in_b[:, c2:c3], name="proj_gate_a")
        g_b = _mm(hb, w_in_b[:, c3:], name="proj_gate_b")

        y_s5 = _s5_branch(u_a, p, l, nseg, lseg, nseg_prompt, ratio)
        s5_out = _mm(y_s5, ssm_w_glu[l].astype(BF16), epilogue=_glu,
                     extras=[(y_s5, "mn"), (ssm_b_glu[l].astype(F32).reshape(1, -1), "n")],
                     out_dtypes=(BF16,), name="s5_glu")

        sw = hy_short_w[l].astype(F32)
        sb = hy_short_b[l].astype(F32)
        x1, x2, v = [_shortconv(u_b, sw, sb, part, hy_w, n_prompt, lp, ls) for part in range(3)]
        y_hy = _hyena_branch(x1, x2, v, p, l, nseg, lseg, nseg_prompt, ratio)

        m_a = _mm(s5_out, w_branch_a[l].astype(BF16), epilogue=_sigmoid_gate, extras=[(g_a, "mn")],
                  name="branch_a")
        merged = _mm(y_hy, w_branch_b[l].astype(BF16), epilogue=_sigmoid_gate_add,
                     extras=[(m_a, "mn"), (g_b, "mn")], out_dtypes=(BF16,), name="branch_b_merge")
        mix = _mm(merged, w_out[l].astype(BF16), name="mixer_out")
        h, hb = _layer_norm(mix, ln1_g[l].astype(F32), ln1_b[l].astype(F32), res=h, alpha=alpha)
        h = _moe_ln(h, p, l, alpha, mo_w_gu[l].astype(BF16), mo_w_down[l].astype(BF16))
        if l + 1 < depth:
            hb = h.astype(BF16)
    y_prompt = h[:n_prompt].reshape(bp, lp, d)
    y_sample = h[n_prompt:].reshape(bs, ls, d)
    return (y_prompt, y_sample)
```

```python
import functools
import math

import numpy as np
import jax
import jax.numpy as jnp
from jax import lax
from jax.experimental import pallas as pl
from jax.experimental.pallas import tpu as pltpu

F32 = jnp.float32
BF16 = jnp.bfloat16
HIGHEST = lax.Precision.HIGHEST

TOP_K = 4
LN_EPS = 1e-5
SWIGLU_ALPHA = 1.702
SWIGLU_LIMIT = 7.0
HY_SHIFT = 0.05

LANES = 128
SUBLANES = 8
VMEM_LIMIT_BYTES = 56 * 1024 * 1024

S5_CHUNK = 32
NEG_BIG = -1e30


def _tile(n, pref, align):
    if n <= pref:
        return n
    t = (pref // align) * align
    while t >= align:
        if n % t == 0:
            return t
        t -= align
    return n


def _params(sem):
    return pltpu.CompilerParams(dimension_semantics=sem, vmem_limit_bytes=VMEM_LIMIT_BYTES)


def _mm_kernel(*refs, nk, n_extra, n_out, epilogue):
    a_ref, b_ref = refs[:2]
    extra = refs[2:2 + n_extra]
    outs = refs[2 + n_extra:2 + n_extra + n_out]
    k = pl.program_id(3)
    part = jnp.dot(a_ref[...], b_ref[...], preferred_element_type=F32)

    def finish(total):
        res = epilogue(total, *[e[...] for e in extra])
        if n_out == 1:
            res = (res,)
        for o, r in zip(outs, res):
            o[...] = r.astype(o.dtype)

    if nk == 1:
        finish(part)
        return
    acc = refs[-1]

    @pl.when(k == 0)
    def _first():
        acc[...] = part

    @pl.when((k > 0) & (k < nk - 1))
    def _middle():
        acc[...] += part

    @pl.when(k == nk - 1)
    def _last():
        finish(acc[...] + part)


def _mm(a, b, *, out_dtypes=(F32,), epilogue=None, extras=(), tm=1024, tn=1024, tk=2048, name="mm"):
    if epilogue is None:
        epilogue = lambda acc: acc
    g_sizes = [x.shape[0] for x in (a, b) if x.ndim == 3]
    g_sizes += [x.shape[0] for x, kind in extras if kind == "mn" and x.ndim == 3]
    G = g_sizes[0] if g_sizes else 1
    M, K = a.shape[-2:]
    N = b.shape[-1]
    tm = _tile(M, tm, 16)
    tn = _tile(N, tn, LANES)
    tk = _tile(K, tk, LANES)
    nk = K // tk

    def spec(x, blk, imap):
        if x.ndim == 3:
            return pl.BlockSpec((None,) + blk, lambda g, i, j, k: (g,) + imap(i, j, k))
        return pl.BlockSpec(blk, lambda g, i, j, k: imap(i, j, k))

    in_specs = [spec(a, (tm, tk), lambda i, j, k: (i, k)),
                spec(b, (tk, tn), lambda i, j, k: (k, j))]
    args = [a, b]
    for x, kind in extras:
        if kind == "mn":
            in_specs.append(spec(x, (tm, tn), lambda i, j, k: (i, j)))
        else:
            in_specs.append(pl.BlockSpec((1, tn), lambda g, i, j, k: (0, j)))
        args.append(x)
    batched = bool(g_sizes)
    out_shape, out_specs = [], []
    for dt in out_dtypes:
        if batched:
            out_shape.append(jax.ShapeDtypeStruct((G, M, N), dt))
            out_specs.append(pl.BlockSpec((None, tm, tn), lambda g, i, j, k: (g, i, j)))
        else:
            out_shape.append(jax.ShapeDtypeStruct((M, N), dt))
            out_specs.append(pl.BlockSpec((tm, tn), lambda g, i, j, k: (i, j)))
    res = pl.pallas_call(
        functools.partial(_mm_kernel, nk=nk, n_extra=len(extras), n_out=len(out_dtypes),
                          epilogue=epilogue),
        grid=(G, M // tm, N // tn, nk),
        in_specs=in_specs,
        out_specs=out_specs,
        out_shape=out_shape,
        scratch_shapes=[pltpu.VMEM((tm, tn), F32)] if nk > 1 else [],
        compiler_params=_params(("parallel", "parallel", "parallel", "arbitrary")),
        name=name,
    )(*args)
    return res[0] if len(out_dtypes) == 1 else res


def _ln_rows(x, g, b):
    mu = jnp.mean(x, axis=-1, keepdims=True)
    xc = x - mu
    var = jnp.mean(xc * xc, axis=-1, keepdims=True)
    return xc * lax.rsqrt(var + LN_EPS) * g + b


def _ln_kernel(*refs, alpha, has_res):
    if has_res:
        x_ref, r_ref, g_ref, b_ref, o_ref, ob_ref = refs
        x = alpha * r_ref[...] + x_ref[...]
    else:
        x_ref, g_ref, b_ref, o_ref, ob_ref = refs
        x = x_ref[...]
    y = _ln_rows(x, g_ref[...], b_ref[...])
    o_ref[...] = y
    ob_ref[...] = y.astype(BF16)


def _layer_norm(x, g, b, res=None, alpha=1.0):
    n, d = x.shape
    tr = _tile(n, 256, 16)
    row = pl.BlockSpec((tr, d), lambda i: (i, 0))
    vec = pl.BlockSpec((1, d), lambda i: (0, 0))
    args = [x] + ([res] if res is not None else []) + [g.reshape(1, d), b.reshape(1, d)]
    in_specs = [row] + ([row] if res is not None else []) + [vec, vec]
    return pl.pallas_call(
        functools.partial(_ln_kernel, alpha=alpha, has_res=res is not None),
        grid=(n // tr,),
        in_specs=in_specs,
        out_specs=[row, row],
        out_shape=[jax.ShapeDtypeStruct((n, d), F32), jax.ShapeDtypeStruct((n, d), BF16)],
        compiler_params=_params(("parallel",)),
        name="layer_norm",
    )(*args)


def _shortconv_kernel(u_ref, prev_ref, next_ref, w_ref, b_ref, o_ref, buf, *, tl, n_prompt, l_prompt, l_sample):
    i = pl.program_id(0)
    r0 = i * tl
    in_prompt = r0 < n_prompt
    seq_len = jnp.where(in_prompt, l_prompt, l_sample)
    off = jnp.where(in_prompt, r0, r0 - n_prompt)
    pos = lax.rem(off, seq_len)
    has_prev = (pos != 0).astype(F32)
    has_next = (pos + tl != seq_len).astype(F32)
    buf[0:SUBLANES, :] = prev_ref[...] * has_prev
    buf[SUBLANES:SUBLANES + tl, :] = u_ref[...]
    buf[SUBLANES + tl:, :] = next_ref[...] * has_next
    w = w_ref[...]
    o_ref[...] = (buf[SUBLANES - 1:SUBLANES - 1 + tl, :] * w[0:1, :]
                  + buf[SUBLANES:SUBLANES + tl, :] * w[1:2, :]
                  + buf[SUBLANES + 1:SUBLANES + 1 + tl, :] * w[2:3, :]
                  + b_ref[...])


def _shortconv(u, w, b, part, width, n_prompt, l_prompt, l_sample):
    n = u.shape[0]
    tl = _tile(math.gcd(l_prompt, l_sample), 512, SUBLANES)
    tc = _tile(width, 1024, LANES)
    ncb = width // tc
    tb = tl // SUBLANES
    nrb = n // SUBLANES
    return pl.pallas_call(
        functools.partial(_shortconv_kernel, tl=tl, n_prompt=n_prompt, l_prompt=l_prompt, l_sample=l_sample),
        grid=(n // tl, ncb),
        in_specs=[
            pl.BlockSpec((tl, tc), lambda i, j: (i, part * ncb + j)),
            pl.BlockSpec((SUBLANES, tc), lambda i, j: (jnp.maximum(i * tb - 1, 0), part * ncb + j)),
            pl.BlockSpec((SUBLANES, tc), lambda i, j: (jnp.minimum((i + 1) * tb, nrb - 1), part * ncb + j)),
            pl.BlockSpec((3, tc), lambda i, j: (0, part * ncb + j)),
            pl.BlockSpec((1, tc), lambda i, j: (0, part * ncb + j)),
        ],
        out_specs=pl.BlockSpec((tl, tc), lambda i, j: (i, j)),
        out_shape=jax.ShapeDtypeStruct((n, width), F32),
        scratch_shapes=[pltpu.VMEM((tl + 2 * SUBLANES, tc), F32)],
        compiler_params=_params(("parallel", "parallel")),
        name="hyena_shortconv",
    )(u, u, u, w, b.reshape(1, -1))


def _s5_in_kernel(u_ref, wm_ref, ws_ref, yi_ref, z_ref, *, thw):
    u = u_ref[...]
    for gi in range(2):
        yi_ref[:, gi * thw:(gi + 1) * thw] = jnp.dot(
            u[:, gi * thw:(gi + 1) * thw], wm_ref[gi], preferred_element_type=F32)
    z_ref[...] = jnp.dot(u, ws_ref[0], preferred_element_type=F32)


def _s5_in(u2, wm, ws, thw):
    r = u2.shape[0]
    npair = wm.shape[0] // 2
    sw = ws.shape[-1]
    tr = _tile(r, 1024, 16)
    return pl.pallas_call(
        functools.partial(_s5_in_kernel, thw=thw),
        grid=(npair, r // tr),
        in_specs=[
            pl.BlockSpec((tr, 2 * thw), lambda p, i: (i, p)),
            pl.BlockSpec((2, thw, thw), lambda p, i: (p, 0, 0)),
            pl.BlockSpec((1, 2 * thw, sw), lambda p, i: (p, 0, 0)),
        ],
        out_specs=[
            pl.BlockSpec((tr, 2 * thw), lambda p, i: (i, p)),
            pl.BlockSpec((tr, sw), lambda p, i: (i, p)),
        ],
        out_shape=[jax.ShapeDtypeStruct((r, npair * 2 * thw), F32),
                   jax.ShapeDtypeStruct((r, npair * sw), F32)],
        compiler_params=_params(("parallel", "parallel")),
        name="s5_chunk_in",
    )(u2, wm, ws)


def _s5_scan_kernel(z_ref, arf_ref, aif_ref, arb_ref, aib_ref, s_ref, *, nchunk, npb, nseg_prompt, ratio):
    q = LANES
    rows = lax.broadcasted_iota(jnp.int32, (SUBLANES, q), 0)
    has_pred = rows < 0
    has_succ = rows < 0
    for r in range(nseg_prompt, SUBLANES):
        if (r - nseg_prompt) % ratio != 0:
            has_pred = has_pred | (rows == r)
        if (r - nseg_prompt) % ratio != ratio - 1:
            has_succ = has_succ | (rows == r)

    def coef(p):
        sl = slice(p * q, (p + 1) * q)
        return (arf_ref[:, sl], aif_ref[:, sl], arb_ref[:, sl], aib_ref[:, sl])

    def lanes(p, k):
        return slice(p * 4 * q + k * q, p * 4 * q + (k + 1) * q)

    def cmul(ar, ai, sr, si):
        return ar * sr - ai * si, ar * si + ai * sr

    zero = jnp.zeros((SUBLANES, q), F32)
    coefs = [coef(p) for p in range(npb)]

    def main_body(c, carry):
        cb = nchunk - 1 - c
        new = []
        for p in range(npb):
            fr, fi, br, bi = carry[4 * p:4 * p + 4]
            arf, aif, arb, aib = coefs[p]
            s_ref[c, :, lanes(p, 0)] = fr
            s_ref[c, :, lanes(p, 1)] = fi
            s_ref[cb, :, lanes(p, 2)] = br
            s_ref[cb, :, lanes(p, 3)] = bi
            nfr, nfi = cmul(arf, aif, fr, fi)
            nbr, nbi = cmul(arb, aib, br, bi)
            new += [nfr + z_ref[c, :, lanes(p, 0)], nfi + z_ref[c, :, lanes(p, 1)],
                    nbr + z_ref[cb, :, lanes(p, 2)], nbi + z_ref[cb, :, lanes(p, 3)]]
        return tuple(new)

    fin = lax.fori_loop(0, nchunk, main_body, tuple([zero] * (4 * npb)))

    def fix_body(c, carry):
        cb = nchunk - 1 - c
        new = []
        for p in range(npb):
            fr, fi, br, bi = carry[4 * p:4 * p + 4]
            arf, aif, arb, aib = coefs[p]
            s_ref[c, :, lanes(p, 0)] += fr
            s_ref[c, :, lanes(p, 1)] += fi
            s_ref[cb, :, lanes(p, 2)] += br
            s_ref[cb, :, lanes(p, 3)] += bi
            new += list(cmul(arf, aif, fr, fi)) + list(cmul(arb, aib, br, bi))
        return tuple(new)

    for _ in range(ratio - 1):
        init = []
        for p in range(npb):
            fr, fi, br, bi = fin[4 * p:4 * p + 4]
            init += [jnp.where(has_pred, pltpu.roll(fr, 1, 0), 0.0),
                     jnp.where(has_pred, pltpu.roll(fi, 1, 0), 0.0),
                     jnp.where(has_succ, pltpu.roll(br, SUBLANES - 1, 0), 0.0),
                     jnp.where(has_succ, pltpu.roll(bi, SUBLANES - 1, 0), 0.0)]
        fin = lax.fori_loop(0, nchunk, fix_body, tuple(init))


def _s5_scan(z3, arf, aif, arb, aib, nseg_prompt, ratio):
    nchunk, nseg, cols = z3.shape
    assert nseg == SUBLANES
    sw = 4 * LANES
    npb = 2 if (cols // sw) % 2 == 0 else 1
    cb = npb * sw
    coef_spec = pl.BlockSpec((1, npb * LANES), lambda j: (0, j))
    blk = pl.BlockSpec((nchunk, nseg, cb), lambda j: (0, 0, j))
    return pl.pallas_call(
        functools.partial(_s5_scan_kernel, nchunk=nchunk, npb=npb, nseg_prompt=nseg_prompt, ratio=ratio),
        grid=(cols // cb,),
        in_specs=[blk, coef_spec, coef_spec, coef_spec, coef_spec],
        out_specs=blk,
        out_shape=jax.ShapeDtypeStruct(z3.shape, F32),
        compiler_params=_params(("parallel",)),
        name="s5_chunk_scan",
    )(z3, arf, aif, arb, aib)


def _gelu_tanh(x):
    return 0.5 * x * (1.0 + jnp.tanh(math.sqrt(2.0 / math.pi) * (x + 0.044715 * (x * x * x))))


def _s5_out_kernel(yi_ref, s_ref, cc_ref, o_ref):
    y = yi_ref[...] + jnp.dot(s_ref[...].astype(BF16), cc_ref[0], preferred_element_type=F32)
    o_ref[...] = _gelu_tanh(y).astype(o_ref.dtype)


def _s5_out(yi, s2, cc, thw):
    r = yi.shape[0]
    npair = cc.shape[0]
    sw = cc.shape[1]
    tr = _tile(r, 1024, 16)
    return pl.pallas_call(
        _s5_out_kernel,
        grid=(npair, r // tr),
        in_specs=[
            pl.BlockSpec((tr, 2 * thw), lambda p, i: (i, p)),
            pl.BlockSpec((tr, sw), lambda p, i: (i, p)),
            pl.BlockSpec((1, sw, 2 * thw), lambda p, i: (p, 0, 0)),
        ],
        out_specs=pl.BlockSpec((tr, 2 * thw), lambda p, i: (i, p)),
        out_shape=jax.ShapeDtypeStruct(yi.shape, BF16),
        compiler_params=_params(("parallel", "parallel")),
        name="s5_chunk_out",
    )(yi, s2, cc)


def _s5_weights(p, l, t):
    a_re, a_im = p["ssm_A_re"][l].astype(F32), p["ssm_A_im"][l].astype(F32)
    ndir, g, ns = a_re.shape
    h = p["ssm_B_re"].shape[-1]
    lam = lax.complex(a_re, a_im)
    step = jnp.exp(p["ssm_log_step"][l].astype(F32))[..., None]
    lam_bar = jnp.exp(lam * step)
    b_bar = ((lam_bar - 1.0) / lam)[..., None] * lax.complex(p["ssm_B_re"][l].astype(F32), p["ssm_B_im"][l].astype(F32))
    c_mat = lax.complex(p["ssm_C_re"][l].astype(F32), p["ssm_C_im"][l].astype(F32))
    jj = jnp.arange(t + 1, dtype=F32)
    pw = jnp.exp((lam * step)[..., None] * jj)
    kf = jnp.real(jnp.einsum("gop,gpd,gpk->gdok", c_mat[0], pw[0, :, :, :t], b_bar[0], precision=HIGHEST))
    kb = jnp.real(jnp.einsum("gop,gpd,gpk->gdok", c_mat[1], pw[1, :, :, :t], b_bar[1], precision=HIGHEST))
    ii = np.arange(t)
    ef = (ii[None, None, :] - ii[None, :, None] == ii[:, None, None]).astype(np.float32)
    eb = (ii[None, :, None] - ii[None, None, :] == ii[:, None, None]).astype(np.float32)
    m = (jnp.einsum("dji,gdok->gjkio", ef, kf, precision=HIGHEST)
         + jnp.einsum("dji,gdok->gjkio", eb, kb, precision=HIGHEST))
    d_skip = p["ssm_D"][l].astype(F32).reshape(g, h)
    eye_t = np.eye(t, dtype=np.float32)
    eye_h = np.eye(h, dtype=np.float32)
    m = m + jnp.einsum("ji,ko,go->gjkio", eye_t, eye_h, d_skip)
    thw = t * h
    wm = m.reshape(g, thw, thw)
    bf = pw[0, :, :, :t][..., ::-1][:, None, :, :] * jnp.transpose(b_bar[0], (0, 2, 1))[..., None]
    bb = pw[1, :, :, :t][:, None, :, :] * jnp.transpose(b_bar[1], (0, 2, 1))[..., None]
    wsg = jnp.stack([jnp.real(bf), jnp.imag(bf), jnp.real(bb), jnp.imag(bb)], axis=0)
    wsg = jnp.transpose(wsg, (1, 4, 2, 0, 3))
    wsg = wsg.reshape(g // 2, 2, thw, 4, ns)
    eye2 = np.eye(2, dtype=np.float32)
    ws = jnp.einsum("narqp,ab->narqbp", wsg, eye2).reshape(g // 2, 2 * thw, 4 * 2 * ns)
    cf = c_mat[0][..., None] * pw[0, :, None, :, 1:t + 1]
    cb = c_mat[1][..., None] * pw[1, :, None, :, 1:t + 1][..., ::-1]
    ccg = jnp.stack([jnp.real(cf), -jnp.imag(cf), jnp.real(cb), -jnp.imag(cb)], axis=0)
    ccg = jnp.transpose(ccg, (1, 0, 3, 4, 2))
    ccg = ccg.reshape(g // 2, 2, 4, ns, thw)
    cc = jnp.einsum("naqpc,ab->nqbpac", ccg, eye2).reshape(g // 2, 4 * 2 * ns, 2 * thw)
    a_t = pw[:, :, :, t]
    coefs = [jnp.real(a_t[0]).reshape(1, -1), jnp.imag(a_t[0]).reshape(1, -1),
             jnp.real(a_t[1]).reshape(1, -1), jnp.imag(a_t[1]).reshape(1, -1)]
    return wm.astype(BF16), ws.astype(BF16), cc.astype(BF16), coefs


def _s5_branch(u_a, p, l, nseg, lseg, nseg_prompt, ratio):
    ntok, width = u_a.shape
    h = p["ssm_B_re"].shape[-1]
    g = width // h
    t = S5_CHUNK
    thw = t * h
    nchunk = lseg // t
    wm, ws, cc, coefs = _s5_weights(p, l, t)
    u2 = u_a.reshape(nseg, nchunk, t, g, h).transpose(1, 0, 3, 2, 4).reshape(nchunk * nseg, g * thw)
    yi, z = _s5_in(u2, wm, ws, thw)
    s3 = _s5_scan(z.reshape(nchunk, nseg, -1), *coefs, nseg_prompt, ratio)
    y2 = _s5_out(yi, s3.reshape(nchunk * nseg, -1), cc, thw)
    return y2.reshape(nchunk, nseg, g, t, h).transpose(1, 0, 3, 2, 4).reshape(ntok, width)


def _filter_kernel(z_ref, w1_ref, b1_ref, w2_ref, b2_ref, w3_ref, b3_ref, w4_ref, fr_ref, dec_ref, o_ref):
    z = z_ref[...]
    fr = fr_ref[...]
    hcur = jnp.sin(fr * (jnp.dot(z, w1_ref[...], precision=HIGHEST, preferred_element_type=F32) + b1_ref[...]))
    hcur = jnp.sin(fr * (jnp.dot(hcur, w2_ref[...], precision=HIGHEST, preferred_element_type=F32) + b2_ref[...]))
    hcur = jnp.sin(fr * (jnp.dot(hcur, w3_ref[...], precision=HIGHEST, preferred_element_type=F32) + b3_ref[...]))
    k = jnp.dot(hcur, w4_ref[...], precision=HIGHEST, preferred_element_type=F32)
    t = z[:, 0:1]
    o_ref[...] = k * (jnp.exp(-t * jnp.abs(dec_ref[...])) + HY_SHIFT)


def _implicit_filters(length, p, l):
    emb = p["hf_w1"].shape[1]
    hid = p["hf_w1"].shape[2]
    bands = (emb - 1) // 2
    tt = jnp.linspace(0.0, 1.0, length, dtype=F32)[:, None]
    w = (2.0 * math.pi / length) * jnp.arange(length, dtype=F32)[:, None]
    f = jnp.linspace(1e-4, bands - 1, bands, dtype=F32)[None, :]
    z = jnp.concatenate([tt, jnp.cos(f * w), -jnp.sin(f * w)], axis=-1)
    zp = jnp.pad(z, ((0, 0), (0, LANES - emb)))
    w1 = jnp.pad(p["hf_w1"][l].astype(F32), ((0, LANES - emb), (0, 0)))
    cols = p["hf_w4"].shape[-1]
    tl = _tile(length, 256, SUBLANES)
    full = lambda shape: pl.BlockSpec(shape, lambda i: (0,) * len(shape))
    return pl.pallas_call(
        _filter_kernel,
        grid=(length // tl,),
        in_specs=[pl.BlockSpec((tl, LANES), lambda i: (i, 0)),
                  full((LANES, hid)), full((1, hid)), full((hid, hid)), full((1, hid)),
                  full((hid, hid)), full((1, hid)), full((hid, cols)), full((1, hid)), full((1, cols))],
        out_specs=pl.BlockSpec((tl, cols), lambda i: (i, 0)),
        out_shape=jax.ShapeDtypeStruct((length, cols), F32),
        compiler_params=_params(("parallel",)),
        name="hyena_filter_mlp",
    )(zp, w1, p["hf_b1"][l].astype(F32).reshape(1, hid), p["hf_w2"][l].astype(F32),
      p["hf_b2"][l].astype(F32).reshape(1, hid), p["hf_w3"][l].astype(F32),
      p["hf_b3"][l].astype(F32).reshape(1, hid), p["hf_w4"][l].astype(F32),
      p["hf_freq"][l].astype(F32).reshape(1, hid), p["hf_decay"][l].astype(F32).reshape(1, cols))


def _dft_tables(lseg, tq):
    n = 2 * lseg
    nfreq = lseg
    tb = 1
    while tb * tb < lseg:
        tb *= 2
    ta = lseg // tb
    k2 = 2 * jnp.arange(nfreq, dtype=jnp.int32)[:, None] + 1

    def cs(tvals):
        m = (k2 * tvals[None, :]) % (2 * n)
        ang = m.astype(F32) * (math.pi / n)
        return jnp.cos(ang), jnp.sin(ang)

    ca, sa = cs(jnp.arange(ta, dtype=jnp.int32) * tb)
    cb, sb = cs(jnp.arange(tb, dtype=jnp.int32))
    cos = (ca[:, :, None] * cb[:, None, :] - sa[:, :, None] * sb[:, None, :]).reshape(nfreq, lseg)
    nsin = -(sa[:, :, None] * cb[:, None, :] + ca[:, :, None] * sb[:, None, :]).reshape(nfreq, lseg)
    cos_b, nsin_b = cos.astype(BF16), nsin.astype(BF16)
    nq = nfreq // tq
    f_fwd = jnp.stack([cos_b.reshape(nq, tq, lseg), nsin_b.reshape(nq, tq, lseg)], axis=1).reshape(n, lseg)
    f_inv = (f_fwd.T.astype(F32) * (2.0 / n)).astype(BF16)
    return cos_b, nsin_b, f_fwd, f_inv


def _spec_mul_kernel(x_ref, kre_ref, kim_ref, o_ref, *, tq, nseg_prompt, ratio, nseq_sample):
    def cmul(xr, xi, kr, ki):
        return xr * kr - xi * ki, xr * ki + xi * kr

    def put(s, yr, yi):
        o_ref[s, 0:tq, :] = yr.astype(o_ref.dtype)
        o_ref[s, tq:2 * tq, :] = yi.astype(o_ref.dtype)

    for s in range(nseg_prompt):
        yr, yi = cmul(x_ref[s, 0:tq, :], x_ref[s, tq:2 * tq, :], kre_ref[0], kim_ref[0])
        put(s, yr, yi)
    for q in range(nseq_sample):
        base = nseg_prompt + q * ratio
        for a in range(ratio):
            acc_r = acc_i = None
            for b in range(ratio):
                kidx = 1 + (a - b) + (ratio - 1)
                yr, yi = cmul(x_ref[base + b, 0:tq, :], x_ref[base + b, tq:2 * tq, :], kre_ref[kidx], kim_ref[kidx])
                acc_r = yr if acc_r is None else acc_r + yr
                acc_i = yi if acc_i is None else acc_i + yi
            put(base + a, acc_r, acc_i)


def _spec_mul(xf, kre, kim, tq, nseg_prompt, ratio):
    nseg, n, w = xf.shape
    nspec = kre.shape[0]
    tc = _tile(w, 512, LANES)
    nseq_sample = (nseg - nseg_prompt) // ratio
    return pl.pallas_call(
        functools.partial(_spec_mul_kernel, tq=tq, nseg_prompt=nseg_prompt, ratio=ratio, nseq_sample=nseq_sample),
        grid=(n // (2 * tq), w // tc),
        in_specs=[pl.BlockSpec((nseg, 2 * tq, tc), lambda i, j: (0, i, j)),
                  pl.BlockSpec((nspec, tq, tc), lambda i, j: (0, i, j)),
                  pl.BlockSpec((nspec, tq, tc), lambda i, j: (0, i, j))],
        out_specs=pl.BlockSpec((nseg, 2 * tq, tc), lambda i, j: (0, i, j)),
        out_shape=jax.ShapeDtypeStruct(xf.shape, BF16),
        compiler_params=_params(("parallel", "parallel")),
        name="hyena_spectrum_mul",
    )(xf, kre, kim)


def _filter_sequences(kp, ks, lseg, ratio, w):
    order = kp.shape[1] // (2 * w)
    kp = kp.reshape(lseg, order, 2, w)
    ks = ks.reshape(ratio * lseg, order, 2, w)
    zero_row = jnp.zeros((1, order, w), F32)
    plus, minus = [], []

    def add(c, a):
        plus.append(c + a)
        minus.append(c - a)

    add(kp[:, :, 0], kp[:, :, 1])
    fwd = jnp.concatenate([ks[:, :, 0], jnp.zeros((lseg, order, w), F32)], axis=0)
    bwd = jnp.concatenate([ks[:, :, 1], jnp.zeros((lseg, order, w), F32)], axis=0)
    for d in range(-(ratio - 1), ratio):
        if d == 0:
            add(fwd[:lseg], bwd[:lseg])
        elif d > 0:
            c = fwd[d * lseg:(d + 1) * lseg]
            a = jnp.concatenate([zero_row, fwd[(d - 1) * lseg + 1:d * lseg][::-1]], axis=0)
            add(c, a)
        else:
            dd = -d
            c = bwd[(dd - 1) * lseg + 1:dd * lseg + 1][::-1]
            a = jnp.concatenate([zero_row, bwd[dd * lseg + 1:(dd + 1) * lseg]], axis=0)
            add(c, a)
    plus = jnp.stack(plus, axis=0).transpose(2, 0, 1, 3)
    minus = jnp.stack(minus, axis=0).transpose(2, 0, 1, 3)
    return plus, minus


def _hyena_branch(x1, x2, v, p, l, nseg, lseg, nseg_prompt, ratio):
    ntok, w = v.shape
    n = 2 * lseg
    tq = _tile(lseg, 256, 16)
    cos_b, nsin_b, f_fwd, f_inv = _dft_tables(lseg, tq)
    kp = _implicit_filters(lseg, p, l)
    ks = _implicit_filters(ratio * lseg, p, l) if ratio > 1 else kp
    plus, minus = _filter_sequences(kp, ks, lseg, ratio, w)
    order, nspec = plus.shape[:2]
    kre = _mm(cos_b, plus.reshape(order * nspec, lseg, w).astype(BF16), name="hyena_filter_dft_re")
    kim = _mm(nsin_b, minus.reshape(order * nspec, lseg, w).astype(BF16), name="hyena_filter_dft_im")
    kre = kre.reshape(order, nspec, lseg, w)
    kim = kim.reshape(order, nspec, lseg, w)
    fb = p["hy_filter_bias"][l].astype(F32)
    gates = (x1, x2)
    y = v
    for o in range(order):
        y3 = y.reshape(nseg, lseg, w)
        xf = _mm(f_fwd, y3.astype(BF16), name="hyena_dft_fwd")
        yf = _spec_mul(xf, kre[o], kim[o], tq, nseg_prompt, ratio)
        last = o == order - 1
        y = _mm(f_inv, yf,
                epilogue=lambda acc, gate, vin, bias: gate * (acc + vin * bias),
                extras=[(gates[o].reshape(nseg, lseg, w), "mn"), (y3, "mn"), (fb[o].reshape(1, w), "n")],
                out_dtypes=(BF16 if last else F32,), name="hyena_dft_inv").reshape(ntok, w)
    return y


def _router_kernel(h_ref, w_ref, b_ref, gate_ref, idx_ref, mask_ref):
    logits = jnp.dot(h_ref[...], w_ref[...], precision=HIGHEST, preferred_element_type=F32) + b_ref[...]
    lane = lax.broadcasted_iota(jnp.int32, logits.shape, 1).astype(F32)
    vals = logits
    tops, idxs = [], []
    mask = jnp.zeros(logits.shape, F32)
    for _ in range(TOP_K):
        m = jnp.max(vals, axis=-1, keepdims=True)
        idx = jnp.min(jnp.where(vals == m, lane, float(LANES)), axis=-1, keepdims=True)
        sel = lane == idx
        mask = jnp.where(sel, 1.0, mask)
        vals = jnp.where(sel, -jnp.inf, vals)
        tops.append(m)
        idxs.append(idx)
    es = [jnp.exp(t - tops[0]) for t in tops]
    denom = es[0]
    for e in es[1:]:
        denom = denom + e
    gate_c = jnp.zeros(logits.shape, F32)
    idx_c = jnp.zeros(logits.shape, F32)
    for k in range(TOP_K):
        gate_c = jnp.where(lane == k, es[k] / denom, gate_c)
        idx_c = jnp.where(lane == k, idxs[k], idx_c)
    gate_ref[...] = gate_c
    idx_ref[...] = idx_c.astype(jnp.int32)
    mask_ref[...] = mask.astype(BF16)


def _router(h, w_router, b_router):
    n, d = h.shape
    e = w_router.shape[-1]
    wp = jnp.pad(w_router.astype(F32), ((0, 0), (0, LANES - e)))
    bp = jnp.pad(b_router.astype(F32), (0, LANES - e), constant_values=NEG_BIG).reshape(1, LANES)
    tr = _tile(n, 256, 16)
    row = pl.BlockSpec((tr, LANES), lambda i: (i, 0))
    return pl.pallas_call(
        _router_kernel,
        grid=(n // tr,),
        in_specs=[pl.BlockSpec((tr, d), lambda i: (i, 0)),
                  pl.BlockSpec((d, LANES), lambda i: (0, 0)),
                  pl.BlockSpec((1, LANES), lambda i: (0, 0))],
        out_specs=[row, row, row],
        out_shape=[jax.ShapeDtypeStruct((n, LANES), F32), jax.ShapeDtypeStruct((n, LANES), jnp.int32),
                   jax.ShapeDtypeStruct((n, LANES), BF16)],
        compiler_params=_params(("parallel",)),
        name="moe_router",
    )(h, wp, bp)


def _rank_kernel(mask_ref, idx_ref, dest_ref, meta_ref, counts, carry, pstart, *, tr, block):
    ps = pl.program_id(0)
    i = pl.program_id(1)
    mask = mask_ref[...]
    colsum = jnp.sum(mask.astype(F32), axis=0, keepdims=True)

    @pl.when((ps == 0) & (i == 0))
    def _zero():
        counts[...] = jnp.zeros_like(counts)

    @pl.when(ps == 0)
    def _count():
        counts[...] += colsum
        dest_ref[...] = jnp.zeros_like(dest_ref)

    @pl.when((ps == 1) & (i == 0))
    def _starts():
        padded = jnp.floor((counts[...] + (block - 1)) / block) * block
        r = lax.broadcasted_iota(jnp.int32, (LANES, LANES), 0)
        c = lax.broadcasted_iota(jnp.int32, (LANES, LANES), 1)
        upper = (r < c).astype(F32)
        start = jnp.dot(jnp.broadcast_to(padded, (SUBLANES, LANES)), upper, precision=HIGHEST,
                        preferred_element_type=F32)[0:1, :]
        pstart[...] = start
        carry[...] = jnp.zeros_like(carry)
        rows = lax.broadcasted_iota(jnp.int32, (SUBLANES, LANES), 0)
        meta_ref[...] = jnp.where(rows == 0, start, jnp.where(rows == 1, start + padded, 0.0))

    @pl.when(ps == 1)
    def _rank():
        r = lax.broadcasted_iota(jnp.int32, (tr, tr), 0)
        c = lax.broadcasted_iota(jnp.int32, (tr, tr), 1)
        lower = (c < r).astype(BF16)
        rank = jnp.dot(lower, mask, preferred_element_type=F32)
        slot = pstart[...] + carry[...] + rank
        lane = lax.broadcasted_iota(jnp.int32, (tr, LANES), 1)
        idx_c = idx_ref[...]
        dest = jnp.zeros((tr, LANES), F32)
        for k in range(TOP_K):
            sel = lane == idx_c[:, k:k + 1]
            dk = jnp.sum(jnp.where(sel, slot, 0.0), axis=-1, keepdims=True)
            dest = jnp.where(lane == k, dk, dest)
        dest_ref[...] = dest.astype(jnp.int32)
        carry[...] += colsum


def _rank(mask, idx_c, block):
    n = mask.shape[0]
    tr = _tile(n, 256, 16)
    row = lambda: pl.BlockSpec((tr, LANES), lambda ps, i: (i, 0))
    return pl.pallas_call(
        functools.partial(_rank_kernel, tr=tr, block=block),
        grid=(2, n // tr),
        in_specs=[row(), row()],
        out_specs=[pl.BlockSpec((tr, LANES), lambda ps, i: (i * ps, 0)),
                   pl.BlockSpec((SUBLANES, LANES), lambda ps, i: (0, 0))],
        out_shape=[jax.ShapeDtypeStruct((n, LANES), jnp.int32), jax.ShapeDtypeStruct((SUBLANES, LANES), F32)],
        scratch_shapes=[pltpu.VMEM((1, LANES), F32), pltpu.VMEM((1, LANES), F32), pltpu.VMEM((1, LANES), F32)],
        compiler_params=_params(("arbitrary", "arbitrary")),
        name="moe_rank",
    )(mask, idx_c)


def _expert_kernel(be_ref, nbu_ref, tok_hbm, h_hbm, wg_ref, wl_ref, bg_ref, bl_ref, wd_ref, bd_ref,
                   out_ref, xf, x2, idx, sem_idx, sem_rows, *, tm, nb, tdn):
    b = pl.program_id(0)
    f = pl.program_id(1)
    d = x2.shape[1]

    def load_idx(blk):
        cp = pltpu.make_async_copy(tok_hbm.at[blk], idx, sem_idx)
        cp.start()
        cp.wait()

    def row_copy(tok, t):
        return pltpu.make_async_copy(h_hbm.at[pl.ds(tok, 1)], xf.at[pl.ds(t, 1)], sem_rows)

    def start_rows():
        def body(t, carry):
            row_copy(idx[t // LANES, t % LANES], t).start()
            return carry
        lax.fori_loop(0, tm, body, 0)

    def wait_rows():
        def body(t, carry):
            row_copy(0, t).wait()
            return carry
        lax.fori_loop(0, tm, body, 0)

    @pl.when((f == 0) & (b == 0))
    def _prime():
        load_idx(0)
        start_rows()

    @pl.when(f == 0)
    def _stage():
        wait_rows()
        x2[...] = xf[...].astype(BF16)

        @pl.when(b + 1 < nb)
        def _next():
            load_idx(b + 1)
            start_rows()

    used = b < nbu_ref[0]

    def compute(first):
        x = x2[...]
        g = jnp.dot(x, wg_ref[0], preferred_element_type=F32) + bg_ref[0]
        lin = jnp.dot(x, wl_ref[0], preferred_element_type=F32) + bl_ref[0]
        g = jnp.minimum(g, SWIGLU_LIMIT)
        lin = jnp.clip(lin, -SWIGLU_LIMIT, SWIGLU_LIMIT)
        act = (g * jax.nn.sigmoid(SWIGLU_ALPHA * g) * (lin + 1.0)).astype(BF16)
        for c in range(d // tdn):
            cols = slice(c * tdn, (c + 1) * tdn)
            part = jnp.dot(act, wd_ref[0, :, cols], preferred_element_type=F32)
            if first:
                out_ref[:, cols] = part + bd_ref[0, :, cols]
            else:
                out_ref[:, cols] += part

    @pl.when(used & (f == 0))
    def _first():
        compute(True)

    @pl.when(used & (f != 0))
    def _rest():
        compute(False)

    @pl.when(jnp.logical_not(used) & (f == 0))
    def _unused():
        out_ref[...] = jnp.zeros_like(out_ref)


def _experts(h, slot_token3, block_expert, nb_used, w_gu, b_gu, w_down, b_down, tm, tf):
    nb = slot_token3.shape[0]
    d = h.shape[1]
    e, _, f2 = w_gu.shape
    ff = f2 // 2
    nf = ff // tf
    tdn = _tile(d, 512, LANES)

    def fidx(b, f, nbu):
        return jnp.where(b < nbu[0], f, nf - 1)

    def eidx(b, be):
        return be[b]

    grid_spec = pltpu.PrefetchScalarGridSpec(
        num_scalar_prefetch=2,
        grid=(nb, nf),
        in_specs=[
            pl.BlockSpec(memory_space=pl.ANY),
            pl.BlockSpec(memory_space=pl.ANY),
            pl.BlockSpec((1, d, tf), lambda b, f, be, nbu: (eidx(b, be), 0, fidx(b, f, nbu))),
            pl.BlockSpec((1, d, tf), lambda b, f, be, nbu: (eidx(b, be), 0, nf + fidx(b, f, nbu))),
            pl.BlockSpec((1, 1, tf), lambda b, f, be, nbu: (eidx(b, be), 0, fidx(b, f, nbu))),
            pl.BlockSpec((1, 1, tf), lambda b, f, be, nbu: (eidx(b, be), 0, nf + fidx(b, f, nbu))),
            pl.BlockSpec((1, tf, d), lambda b, f, be, nbu: (eidx(b, be), fidx(b, f, nbu), 0)),
            pl.BlockSpec((1, 1, d), lambda b, f, be, nbu: (eidx(b, be), 0, 0)),
        ],
        out_specs=pl.BlockSpec((tm, d), lambda b, f, be, nbu: (b, 0)),
        scratch_shapes=[
            pltpu.VMEM((tm, d), F32),
            pltpu.VMEM((tm, d), BF16),
            pltpu.SMEM(slot_token3.shape[1:], jnp.int32),
            pltpu.SemaphoreType.DMA,
            pltpu.SemaphoreType.DMA,
        ],
    )
    return pl.pallas_call(
        functools.partial(_expert_kernel, tm=tm, nb=nb, tdn=tdn),
        grid_spec=grid_spec,
        out_shape=jax.ShapeDtypeStruct((nb * tm, d), F32),
        compiler_params=_params(("arbitrary", "arbitrary")),
        name="moe_experts",
    )(block_expert, nb_used, slot_token3, h, w_gu, w_gu, b_gu, b_gu, w_down, b_down)


def _combine_kernel(dest_hbm, o_hbm, gate_ref, h_ref, g_ref, b_ref, y_ref, buf, pre, idx, sem_idx, sem_rows,
                    *, td, ntile, tdn, alpha):
    i = pl.program_id(0)
    cur = i % 2
    d = pre.shape[1]

    def row_copy(row, s, k, t):
        return pltpu.make_async_copy(o_hbm.at[pl.ds(row, 1)], buf.at[s, k, pl.ds(t, 1)], sem_rows.at[s])

    def fetch(tile, s):
        cp = pltpu.make_async_copy(dest_hbm.at[tile], idx.at[s], sem_idx)
        cp.start()
        cp.wait()

        def body(t, carry):
            for k in range(TOP_K):
                flat = t * TOP_K + k
                row_copy(idx[s, flat // LANES, flat % LANES], s, k, t).start(priority=k % 2)
            return carry

        lax.fori_loop(0, td, body, 0)

    @pl.when(i == 0)
    def _prime():
        fetch(0, 0)

    @pl.when(i + 1 < ntile)
    def _next():
        fetch(i + 1, 1 - cur)

    def wait_body(t, carry):
        for k in range(TOP_K):
            row_copy(0, cur, k, t).wait()
        return carry

    lax.fori_loop(0, td, wait_body, 0)
    gates = gate_ref[...]
    gk = [gates[:, k:k + 1] for k in range(TOP_K)]
    for c in range(d // tdn):
        cols = slice(c * tdn, (c + 1) * tdn)
        moe = gk[0] * buf[cur, 0, :, cols]
        for k in range(1, TOP_K):
            moe = moe + gk[k] * buf[cur, k, :, cols]
        pre[:, cols] = alpha * h_ref[:, cols] + moe
    y_ref[...] = _ln_rows(pre[...], g_ref[...], b_ref[...])


def _combine_ln(dest3, out2, gate_c, h, ln_g, ln_b, alpha):
    n, d = h.shape
    ntile = dest3.shape[0]
    td = n // ntile
    tdn = _tile(d, 512, LANES)
    return pl.pallas_call(
        functools.partial(_combine_kernel, td=td, ntile=ntile, tdn=tdn, alpha=alpha),
        grid=(ntile,),
        in_specs=[
            pl.BlockSpec(memory_space=pl.ANY),
            pl.BlockSpec(memory_space=pl.ANY),
            pl.BlockSpec((td, LANES), lambda i: (i, 0)),
            pl.BlockSpec((td, d), lambda i: (i, 0)),
            pl.BlockSpec((1, d), lambda i: (0, 0)),
            pl.BlockSpec((1, d), lambda i: (0, 0)),
        ],
        out_specs=pl.BlockSpec((td, d), lambda i: (i, 0)),
        out_shape=jax.ShapeDtypeStruct((n, d), F32),
        scratch_shapes=[
            pltpu.VMEM((2, TOP_K, td, d), F32),
            pltpu.VMEM((td, d), F32),
            pltpu.SMEM((2,) + dest3.shape[1:], jnp.int32),
            pltpu.SemaphoreType.DMA,
            pltpu.SemaphoreType.DMA((2,)),
        ],
        compiler_params=_params(("arbitrary",)),
        name="moe_combine_ln",
    )(dest3, out2, gate_c, h, ln_g.reshape(1, d), ln_b.reshape(1, d))


def _moe_ln(h, p, l, alpha, w_gu_b, w_down_b):
    n, d = h.shape
    e = p["mo_w_router"].shape[-1]
    tm = min(512, max(LANES, n * TOP_K // 8))
    td = _tile(n, 128, 32)
    ff = p["mo_w_down"].shape[2]
    tf = _tile(ff, 256, LANES)
    gate_c, idx_c, mask = _router(h, p["mo_w_router"][l], p["mo_b_router"][l])
    dest_c, meta = _rank(mask, idx_c, tm)
    nb = -(-(n * TOP_K) // tm) + e
    n_slots = nb * tm
    dest = dest_c[:, :TOP_K].reshape(-1)
    token_flat = jnp.arange(n * TOP_K, dtype=jnp.int32) // TOP_K
    slot_token = jnp.zeros((n_slots,), jnp.int32).at[dest].set(token_flat)
    pend = meta[1, :e].astype(jnp.int32)
    block_start = jnp.arange(nb, dtype=jnp.int32) * tm
    nb_used = pend[e - 1] // tm
    block_expert = jnp.minimum(jnp.searchsorted(pend, block_start, side="right"), e - 1).astype(jnp.int32)
    last_expert = block_expert[jnp.maximum(nb_used - 1, 0)]
    block_expert = jnp.where(block_start < pend[e - 1], block_expert, last_expert)
    out3 = _experts(h, slot_token.reshape(nb, tm // LANES, LANES), block_expert, nb_used.reshape(1),
                    w_gu_b, p["mo_b_gu"][l].astype(F32).reshape(e, 1, -1),
                    w_down_b, p["mo_b_down"][l].astype(F32).reshape(e, 1, -1), tm, tf)
    dest3 = dest.reshape(n // td, td * TOP_K // LANES, LANES)
    return _combine_ln(dest3, out3, gate_c, h, p["ln2_g"][l].astype(F32), p["ln2_b"][l].astype(F32), alpha)


def _sigmoid_gate(acc, gate):
    return jax.nn.sigmoid(gate) * acc


def _sigmoid_gate_add(acc, prev, gate):
    return prev + jax.nn.sigmoid(gate) * acc


def _glu(acc, y, bias):
    return y.astype(F32) * jax.nn.sigmoid(acc + bias)


def kernel(x_prompt, x_sample, ln0_g, ln0_b, w_in, ssm_A_re, ssm_A_im, ssm_log_step, ssm_B_re, ssm_B_im, ssm_C_re, ssm_C_im, ssm_D, ssm_w_glu, ssm_b_glu, hy_short_w, hy_short_b, hf_w1, hf_b1, hf_w2, hf_b2, hf_w3, hf_b3, hf_w4, hf_freq, hf_decay, hy_filter_bias, w_branch_a, w_branch_b, w_out, ln1_g, ln1_b, mo_w_router, mo_b_router, mo_w_gu, mo_b_gu, mo_w_down, mo_b_down, ln2_g, ln2_b):
    p = dict(ssm_A_re=ssm_A_re, ssm_A_im=ssm_A_im, ssm_log_step=ssm_log_step, ssm_B_re=ssm_B_re,
             ssm_B_im=ssm_B_im, ssm_C_re=ssm_C_re, ssm_C_im=ssm_C_im, ssm_D=ssm_D,
             hf_w1=hf_w1, hf_b1=hf_b1, hf_w2=hf_w2, hf_b2=hf_b2, hf_w3=hf_w3, hf_b3=hf_b3, hf_w4=hf_w4,
             hf_freq=hf_freq, hf_decay=hf_decay, hy_filter_bias=hy_filter_bias,
             mo_w_router=mo_w_router, mo_b_router=mo_b_router, mo_b_gu=mo_b_gu, mo_w_down=mo_w_down,
             mo_b_down=mo_b_down, ln2_g=ln2_g, ln2_b=ln2_b)
    bp, lp, d = x_prompt.shape
    bs, ls, _ = x_sample.shape
    depth = w_in.shape[0]
    alpha = (2.0 * depth) ** 0.25
    lseg = min(lp, ls)
    assert lp == lseg and ls % lseg == 0, "sample sequences must be whole multiples of the prompt length"
    ratio = ls // lseg
    nseg_prompt = bp
    nseg = bp + bs * ratio
    assert nseg == SUBLANES, "the S5 scan keeps one segment per sublane"
    n_prompt = bp * lp
    ssm_w = ssm_D.shape[-1]
    hy_w = hy_filter_bias.shape[-1]
    c1 = ssm_w
    c2 = c1 + 3 * hy_w
    c3 = c2 + d

    x_all = jnp.concatenate([x_prompt.reshape(-1, d), x_sample.reshape(-1, d)], axis=0)
    h, hb = _layer_norm(x_all, ln0_g.astype(F32), ln0_b.astype(F32))
    for l in range(depth):
        w_in_b = w_in[l].astype(BF16)
        u_a = _mm(hb, w_in_b[:, :c1], out_dtypes=(BF16,), name="proj_s5")
        u_b = _mm(hb, w_in_b[:, c1:c2], name="proj_hyena")
        g_a = _mm(hb, w_in_b[:, c2:c3], name="proj_gate_a")
        g_b = _mm(hb, w_in_b[:, c3:], name="proj_gate_b")

        y_s5 = _s5_branch(u_a, p, l, nseg, lseg, nseg_prompt, ratio)
        s5_out = _mm(y_s5, ssm_w_glu[l].astype(BF16), epilogue=_glu,
                     extras=[(y_s5, "mn"), (ssm_b_glu[l].astype(F32).reshape(1, -1), "n")],
                     out_dtypes=(BF16,), name="s5_glu")

        sw = hy_short_w[l].astype(F32)
        sb = hy_short_b[l].astype(F32)
        x1, x2, v = [_shortconv(u_b, sw, sb, part, hy_w, n_prompt, lp, ls) for part in range(3)]
        y_hy = _hyena_branch(x1, x2, v, p, l, nseg, lseg, nseg_prompt, ratio)

        m_a = _mm(s5_out, w_branch_a[l].astype(BF16), epilogue=_sigmoid_gate, extras=[(g_a, "mn")],
                  name="branch_a")
        merged = _mm(y_hy, w_branch_b[l].astype(BF16), epilogue=_sigmoid_gate_add,
                     extras=[(m_a, "mn"), (g_b, "mn")], out_dtypes=(BF16,), name="branch_b_merge")
        mix = _mm(merged, w_out[l].astype(BF16), name="mixer_out")
        h, hb = _layer_norm(mix, ln1_g[l].astype(F32), ln1_b[l].astype(F32), res=h, alpha=alpha)
        h = _moe_ln(h, p, l, alpha, mo_w_gu[l].astype(BF16), mo_w_down[l].astype(BF16))
        if l + 1 < depth:
            hb = h.astype(BF16)
    y_prompt = h[:n_prompt].reshape(bp, lp, d)
    y_sample = h[n_prompt:].reshape(bs, ls, d)
    return (y_prompt, y_sample)
```

```python
import functools
import math

import numpy as np
import jax
import jax.numpy as jnp
from jax import lax
from jax.experimental import pallas as pl
from jax.experimental.pallas import tpu as pltpu

F32 = jnp.float32
BF16 = jnp.bfloat16
HIGHEST = lax.Precision.HIGHEST

TOP_K = 4
LN_EPS = 1e-5
SWIGLU_ALPHA = 1.702
SWIGLU_LIMIT = 7.0
HY_SHIFT = 0.05

LANES = 128
SUBLANES = 8
VMEM_LIMIT_BYTES = 56 * 1024 * 1024

S5_CHUNK = 32
NEG_BIG = -1e30


def _tile(n, pref, align):
    if n <= pref:
        return n
    t = (pref // align) * align
    while t >= align:
        if n % t == 0:
            return t
        t -= align
    return n


def _params(sem):
    return pltpu.CompilerParams(dimension_semantics=sem, vmem_limit_bytes=VMEM_LIMIT_BYTES)


def _mm_kernel(*refs, nk, n_extra, n_out, epilogue):
    a_ref, b_ref = refs[:2]
    extra = refs[2:2 + n_extra]
    outs = refs[2 + n_extra:2 + n_extra + n_out]
    k = pl.program_id(3)
    part = jnp.dot(a_ref[...], b_ref[...], preferred_element_type=F32)

    def finish(total):
        res = epilogue(total, *[e[...] for e in extra])
        if n_out == 1:
            res = (res,)
        for o, r in zip(outs, res):
            o[...] = r.astype(o.dtype)

    if nk == 1:
        finish(part)
        return
    acc = refs[-1]

    @pl.when(k == 0)
    def _first():
        acc[...] = part

    @pl.when((k > 0) & (k < nk - 1))
    def _middle():
        acc[...] += part

    @pl.when(k == nk - 1)
    def _last():
        finish(acc[...] + part)


def _mm(a, b, *, out_dtypes=(F32,), epilogue=None, extras=(), tm=1024, tn=1024, tk=2048, name="mm"):
    if epilogue is None:
        epilogue = lambda acc: acc
    g_sizes = [x.shape[0] for x in (a, b) if x.ndim == 3]
    g_sizes += [x.shape[0] for x, kind in extras if kind == "mn" and x.ndim == 3]
    G = g_sizes[0] if g_sizes else 1
    M, K = a.shape[-2:]
    N = b.shape[-1]
    tm = _tile(M, tm, 16)
    tn = _tile(N, tn, LANES)
    tk = _tile(K, tk, LANES)
    nk = K // tk

    def spec(x, blk, imap):
        if x.ndim == 3:
            return pl.BlockSpec((None,) + blk, lambda g, i, j, k: (g,) + imap(i, j, k))
        return pl.BlockSpec(blk, lambda g, i, j, k: imap(i, j, k))

    in_specs = [spec(a, (tm, tk), lambda i, j, k: (i, k)),
                spec(b, (tk, tn), lambda i, j, k: (k, j))]
    args = [a, b]
    for x, kind in extras:
        if kind == "mn":
            in_specs.append(spec(x, (tm, tn), lambda i, j, k: (i, j)))
        else:
            in_specs.append(pl.BlockSpec((1, tn), lambda g, i, j, k: (0, j)))
        args.append(x)
    batched = bool(g_sizes)
    out_shape, out_specs = [], []
    for dt in out_dtypes:
        if batched:
            out_shape.append(jax.ShapeDtypeStruct((G, M, N), dt))
            out_specs.append(pl.BlockSpec((None, tm, tn), lambda g, i, j, k: (g, i, j)))
        else:
            out_shape.append(jax.ShapeDtypeStruct((M, N), dt))
            out_specs.append(pl.BlockSpec((tm, tn), lambda g, i, j, k: (i, j)))
    res = pl.pallas_call(
        functools.partial(_mm_kernel, nk=nk, n_extra=len(extras), n_out=len(out_dtypes),
                          epilogue=epilogue),
        grid=(G, M // tm, N // tn, nk),
        in_specs=in_specs,
        out_specs=out_specs,
        out_shape=out_shape,
        scratch_shapes=[pltpu.VMEM((tm, tn), F32)] if nk > 1 else [],
        compiler_params=_params(("parallel", "parallel", "parallel", "arbitrary")),
        name=name,
    )(*args)
    return res[0] if len(out_dtypes) == 1 else res


def _ln_rows(x, g, b):
    mu = jnp.mean(x, axis=-1, keepdims=True)
    xc = x - mu
    var = jnp.mean(xc * xc, axis=-1, keepdims=True)
    return xc * lax.rsqrt(var + LN_EPS) * g + b


def _ln_kernel(*refs, alpha, has_res):
    if has_res:
        x_ref, r_ref, g_ref, b_ref, o_ref, ob_ref = refs
        x = alpha * r_ref[...] + x_ref[...]
    else:
        x_ref, g_ref, b_ref, o_ref, ob_ref = refs
        x = x_ref[...]
    y = _ln_rows(x, g_ref[...], b_ref[...])
    o_ref[...] = y
    ob_ref[...] = y.astype(BF16)


def _layer_norm(x, g, b, res=None, alpha=1.0):
    n, d = x.shape
    tr = _tile(n, 256, 16)
    row = pl.BlockSpec((tr, d), lambda i: (i, 0))
    vec = pl.BlockSpec((1, d), lambda i: (0, 0))
    args = [x] + ([res] if res is not None else []) + [g.reshape(1, d), b.reshape(1, d)]
    in_specs = [row] + ([row] if res is not None else []) + [vec, vec]
    return pl.pallas_call(
        functools.partial(_ln_kernel, alpha=alpha, has_res=res is not None),
        grid=(n // tr,),
        in_specs=in_specs,
        out_specs=[row, row],
        out_shape=[jax.ShapeDtypeStruct((n, d), F32), jax.ShapeDtypeStruct((n, d), BF16)],
        compiler_params=_params(("parallel",)),
        name="layer_norm",
    )(*args)


def _shortconv_kernel(u_ref, prev_ref, next_ref, w_ref, b_ref, o_ref, buf, *, tl, n_prompt, l_prompt, l_sample):
    i = pl.program_id(0)
    r0 = i * tl
    in_prompt = r0 < n_prompt
    seq_len = jnp.where(in_prompt, l_prompt, l_sample)
    off = jnp.where(in_prompt, r0, r0 - n_prompt)
    pos = lax.rem(off, seq_len)
    has_prev = (pos != 0).astype(F32)
    has_next = (pos + tl != seq_len).astype(F32)
    buf[0:SUBLANES, :] = prev_ref[...] * has_prev
    buf[SUBLANES:SUBLANES + tl, :] = u_ref[...]
    buf[SUBLANES + tl:, :] = next_ref[...] * has_next
    w = w_ref[...]
    o_ref[...] = (buf[SUBLANES - 1:SUBLANES - 1 + tl, :] * w[0:1, :]
                  + buf[SUBLANES:SUBLANES + tl, :] * w[1:2, :]
                  + buf[SUBLANES + 1:SUBLANES + 1 + tl, :] * w[2:3, :]
                  + b_ref[...])


def _shortconv(u, w, b, part, width, n_prompt, l_prompt, l_sample):
    n = u.shape[0]
    tl = _tile(math.gcd(l_prompt, l_sample), 512, SUBLANES)
    tc = _tile(width, 1024, LANES)
    ncb = width // tc
    tb = tl // SUBLANES
    nrb = n // SUBLANES
    return pl.pallas_call(
        functools.partial(_shortconv_kernel, tl=tl, n_prompt=n_prompt, l_prompt=l_prompt, l_sample=l_sample),
        grid=(n // tl, ncb),
        in_specs=[
            pl.BlockSpec((tl, tc), lambda i, j: (i, part * ncb + j)),
            pl.BlockSpec((SUBLANES, tc), lambda i, j: (jnp.maximum(i * tb - 1, 0), part * ncb + j)),
            pl.BlockSpec((SUBLANES, tc), lambda i, j: (jnp.minimum((i + 1) * tb, nrb - 1), part * ncb + j)),
            pl.BlockSpec((3, tc), lambda i, j: (0, part * ncb + j)),
            pl.BlockSpec((1, tc), lambda i, j: (0, part * ncb + j)),
        ],
        out_specs=pl.BlockSpec((tl, tc), lambda i, j: (i, j)),
        out_shape=jax.ShapeDtypeStruct((n, width), F32),
        scratch_shapes=[pltpu.VMEM((tl + 2 * SUBLANES, tc), F32)],
        compiler_params=_params(("parallel", "parallel")),
        name="hyena_shortconv",
    )(u, u, u, w, b.reshape(1, -1))


def _s5_in_kernel(u_ref, wm_ref, ws_ref, yi_ref, z_ref, *, thw):
    u = u_ref[...]
    for gi in range(2):
        yi_ref[:, gi * thw:(gi + 1) * thw] = jnp.dot(
            u[:, gi * thw:(gi + 1) * thw], wm_ref[gi], preferred_element_type=F32)
    z_ref[...] = jnp.dot(u, ws_ref[0], preferred_element_type=F32)


def _s5_in(u2, wm, ws, thw):
    r = u2.shape[0]
    npair = wm.shape[0] // 2
    sw = ws.shape[-1]
    tr = _tile(r, 1024, 16)
    return pl.pallas_call(
        functools.partial(_s5_in_kernel, thw=thw),
        grid=(npair, r // tr),
        in_specs=[
            pl.BlockSpec((tr, 2 * thw), lambda p, i: (i, p)),
            pl.BlockSpec((2, thw, thw), lambda p, i: (p, 0, 0)),
            pl.BlockSpec((1, 2 * thw, sw), lambda p, i: (p, 0, 0)),
        ],
        out_specs=[
            pl.BlockSpec((tr, 2 * thw), lambda p, i: (i, p)),
            pl.BlockSpec((tr, sw), lambda p, i: (i, p)),
        ],
        out_shape=[jax.ShapeDtypeStruct((r, npair * 2 * thw), F32),
                   jax.ShapeDtypeStruct((r, npair * sw), F32)],
        compiler_params=_params(("parallel", "parallel")),
        name="s5_chunk_in",
    )(u2, wm, ws)


def _s5_scan_kernel(z_ref, arf_ref, aif_ref, arb_ref, aib_ref, s_ref, *, nchunk, npb, nseg_prompt, ratio):
    q = LANES
    rows = lax.broadcasted_iota(jnp.int32, (SUBLANES, q), 0)
    has_pred = rows < 0
    has_succ = rows < 0
    for r in range(nseg_prompt, SUBLANES):
        if (r - nseg_prompt) % ratio != 0:
            has_pred = has_pred | (rows == r)
        if (r - nseg_prompt) % ratio != ratio - 1:
            has_succ = has_succ | (rows == r)

    def coef(p):
        sl = slice(p * q, (p + 1) * q)
        return (arf_ref[:, sl], aif_ref[:, sl], arb_ref[:, sl], aib_ref[:, sl])

    def lanes(p, k):
        return slice(p * 4 * q + k * q, p * 4 * q + (k + 1) * q)

    def cmul(ar, ai, sr, si):
        return ar * sr - ai * si, ar * si + ai * sr

    zero = jnp.zeros((SUBLANES, q), F32)
    coefs = [coef(p) for p in range(npb)]

    def main_body(c, carry):
        cb = nchunk - 1 - c
        new = []
        for p in range(npb):
            fr, fi, br, bi = carry[4 * p:4 * p + 4]
            arf, aif, arb, aib = coefs[p]
            s_ref[c, :, lanes(p, 0)] = fr
            s_ref[c, :, lanes(p, 1)] = fi
            s_ref[cb, :, lanes(p, 2)] = br
            s_ref[cb, :, lanes(p, 3)] = bi
            nfr, nfi = cmul(arf, aif, fr, fi)
            nbr, nbi = cmul(arb, aib, br, bi)
            new += [nfr + z_ref[c, :, lanes(p, 0)], nfi + z_ref[c, :, lanes(p, 1)],
                    nbr + z_ref[cb, :, lanes(p, 2)], nbi + z_ref[cb, :, lanes(p, 3)]]
        return tuple(new)

    fin = lax.fori_loop(0, nchunk, main_body, tuple([zero] * (4 * npb)))

    def fix_body(c, carry):
        cb = nchunk - 1 - c
        new = []
        for p in range(npb):
            fr, fi, br, bi = carry[4 * p:4 * p + 4]
            arf, aif, arb, aib = coefs[p]
            s_ref[c, :, lanes(p, 0)] += fr
            s_ref[c, :, lanes(p, 1)] += fi
            s_ref[cb, :, lanes(p, 2)] += br
            s_ref[cb, :, lanes(p, 3)] += bi
            new += list(cmul(arf, aif, fr, fi)) + list(cmul(arb, aib, br, bi))
        return tuple(new)

    for _ in range(ratio - 1):
        init = []
        for p in range(npb):
            fr, fi, br, bi = fin[4 * p:4 * p + 4]
            init += [jnp.where(has_pred, pltpu.roll(fr, 1, 0), 0.0),
                     jnp.where(has_pred, pltpu.roll(fi, 1, 0), 0.0),
                     jnp.where(has_succ, pltpu.roll(br, SUBLANES - 1, 0), 0.0),
                     jnp.where(has_succ, pltpu.roll(bi, SUBLANES - 1, 0), 0.0)]
        fin = lax.fori_loop(0, nchunk, fix_body, tuple(init))


def _s5_scan(z3, arf, aif, arb, aib, nseg_prompt, ratio):
    nchunk, nseg, cols = z3.shape
    assert nseg == SUBLANES
    sw = 4 * LANES
    npb = 2 if (cols // sw) % 2 == 0 else 1
    cb = npb * sw
    coef_spec = pl.BlockSpec((1, npb * LANES), lambda j: (0, j))
    blk = pl.BlockSpec((nchunk, nseg, cb), lambda j: (0, 0, j))
    return pl.pallas_call(
        functools.partial(_s5_scan_kernel, nchunk=nchunk, npb=npb, nseg_prompt=nseg_prompt, ratio=ratio),
        grid=(cols // cb,),
        in_specs=[blk, coef_spec, coef_spec, coef_spec, coef_spec],
        out_specs=blk,
        out_shape=jax.ShapeDtypeStruct(z3.shape, F32),
        compiler_params=_params(("parallel",)),
        name="s5_chunk_scan",
    )(z3, arf, aif, arb, aib)


def _gelu_tanh(x):
    return 0.5 * x * (1.0 + jnp.tanh(math.sqrt(2.0 / math.pi) * (x + 0.044715 * (x * x * x))))


def _s5_out_kernel(yi_ref, s_ref, cc_ref, o_ref):
    y = yi_ref[...] + jnp.dot(s_ref[...].astype(BF16), cc_ref[0], preferred_element_type=F32)
    o_ref[...] = _gelu_tanh(y).astype(o_ref.dtype)


def _s5_out(yi, s2, cc, thw):
    r = yi.shape[0]
    npair = cc.shape[0]
    sw = cc.shape[1]
    tr = _tile(r, 1024, 16)
    return pl.pallas_call(
        _s5_out_kernel,
        grid=(npair, r // tr),
        in_specs=[
            pl.BlockSpec((tr, 2 * thw), lambda p, i: (i, p)),
            pl.BlockSpec((tr, sw), lambda p, i: (i, p)),
            pl.BlockSpec((1, sw, 2 * thw), lambda p, i: (p, 0, 0)),
        ],
        out_specs=pl.BlockSpec((tr, 2 * thw), lambda p, i: (i, p)),
        out_shape=jax.ShapeDtypeStruct(yi.shape, BF16),
        compiler_params=_params(("parallel", "parallel")),
        name="s5_chunk_out",
    )(yi, s2, cc)


def _s5_weights(p, l, t):
    a_re, a_im = p["ssm_A_re"][l].astype(F32), p["ssm_A_im"][l].astype(F32)
    ndir, g, ns = a_re.shape
    h = p["ssm_B_re"].shape[-1]
    lam = lax.complex(a_re, a_im)
    step = jnp.exp(p["ssm_log_step"][l].astype(F32))[..., None]
    lam_bar = jnp.exp(lam * step)
    b_bar = ((lam_bar - 1.0) / lam)[..., None] * lax.complex(p["ssm_B_re"][l].astype(F32), p["ssm_B_im"][l].astype(F32))
    c_mat = lax.complex(p["ssm_C_re"][l].astype(F32), p["ssm_C_im"][l].astype(F32))
    jj = jnp.arange(t + 1, dtype=F32)
    pw = jnp.exp((lam * step)[..., None] * jj)
    kf = jnp.real(jnp.einsum("gop,gpd,gpk->gdok", c_mat[0], pw[0, :, :, :t], b_bar[0], precision=HIGHEST))
    kb = jnp.real(jnp.einsum("gop,gpd,gpk->gdok", c_mat[1], pw[1, :, :, :t], b_bar[1], precision=HIGHEST))
    ii = np.arange(t)
    ef = (ii[None, None, :] - ii[None, :, None] == ii[:, None, None]).astype(np.float32)
    eb = (ii[None, :, None] - ii[None, None, :] == ii[:, None, None]).astype(np.float32)
    m = (jnp.einsum("dji,gdok->gjkio", ef, kf, precision=HIGHEST)
         + jnp.einsum("dji,gdok->gjkio", eb, kb, precision=HIGHEST))
    d_skip = p["ssm_D"][l].astype(F32).reshape(g, h)
    eye_t = np.eye(t, dtype=np.float32)
    eye_h = np.eye(h, dtype=np.float32)
    m = m + jnp.einsum("ji,ko,go->gjkio", eye_t, eye_h, d_skip)
    thw = t * h
    wm = m.reshape(g, thw, thw)
    bf = pw[0, :, :, :t][..., ::-1][:, None, :, :] * jnp.transpose(b_bar[0], (0, 2, 1))[..., None]
    bb = pw[1, :, :, :t][:, None, :, :] * jnp.transpose(b_bar[1], (0, 2, 1))[..., None]
    wsg = jnp.stack([jnp.real(bf), jnp.imag(bf), jnp.real(bb), jnp.imag(bb)], axis=0)
    wsg = jnp.transpose(wsg, (1, 4, 2, 0, 3))
    wsg = wsg.reshape(g // 2, 2, thw, 4, ns)
    eye2 = np.eye(2, dtype=np.float32)
    ws = jnp.einsum("narqp,ab->narqbp", wsg, eye2).reshape(g // 2, 2 * thw, 4 * 2 * ns)
    cf = c_mat[0][..., None] * pw[0, :, None, :, 1:t + 1]
    cb = c_mat[1][..., None] * pw[1, :, None, :, 1:t + 1][..., ::-1]
    ccg = jnp.stack([jnp.real(cf), -jnp.imag(cf), jnp.real(cb), -jnp.imag(cb)], axis=0)
    ccg = jnp.transpose(ccg, (1, 0, 3, 4, 2))
    ccg = ccg.reshape(g // 2, 2, 4, ns, thw)
    cc = jnp.einsum("naqpc,ab->nqbpac", ccg, eye2).reshape(g // 2, 4 * 2 * ns, 2 * thw)
    a_t = pw[:, :, :, t]
    coefs = [jnp.real(a_t[0]).reshape(1, -1), jnp.imag(a_t[0]).reshape(1, -1),
             jnp.real(a_t[1]).reshape(1, -1), jnp.imag(a_t[1]).reshape(1, -1)]
    return wm.astype(BF16), ws.astype(BF16), cc.astype(BF16), coefs


def _s5_branch(u_a, p, l, nseg, lseg, nseg_prompt, ratio):
    ntok, width = u_a.shape
    h = p["ssm_B_re"].shape[-1]
    g = width // h
    t = S5_CHUNK
    thw = t * h
    nchunk = lseg // t
    wm, ws, cc, coefs = _s5_weights(p, l, t)
    u2 = u_a.reshape(nseg, nchunk, t, g, h).transpose(1, 0, 3, 2, 4).reshape(nchunk * nseg, g * thw)
    yi, z = _s5_in(u2, wm, ws, thw)
    s3 = _s5_scan(z.reshape(nchunk, nseg, -1), *coefs, nseg_prompt, ratio)
    y2 = _s5_out(yi, s3.reshape(nchunk * nseg, -1), cc, thw)
    return y2.reshape(nchunk, nseg, g, t, h).transpose(1, 0, 3, 2, 4).reshape(ntok, width)


def _filter_kernel(z_ref, w1_ref, b1_ref, w2_ref, b2_ref, w3_ref, b3_ref, w4_ref, fr_ref, dec_ref, o_ref):
    z = z_ref[...]
    fr = fr_ref[...]
    hcur = jnp.sin(fr * (jnp.dot(z, w1_ref[...], precision=HIGHEST, preferred_element_type=F32) + b1_ref[...]))
    hcur = jnp.sin(fr * (jnp.dot(hcur, w2_ref[...], precision=HIGHEST, preferred_element_type=F32) + b2_ref[...]))
    hcur = jnp.sin(fr * (jnp.dot(hcur, w3_ref[...], precision=HIGHEST, preferred_element_type=F32) + b3_ref[...]))
    k = jnp.dot(hcur, w4_ref[...], precision=HIGHEST, preferred_element_type=F32)
    t = z[:, 0:1]
    o_ref[...] = k * (jnp.exp(-t * jnp.abs(dec_ref[...])) + HY_SHIFT)


def _implicit_filters(pos, length, p, l):
    emb = p["hf_w1"].shape[1]
    hid = p["hf_w1"].shape[2]
    bands = (emb - 1) // 2
    n = pos.shape[0]
    posf = pos.astype(F32)[:, None]
    tt = posf / (length - 1)
    w = (2.0 * math.pi / length) * posf
    f = jnp.linspace(1e-4, bands - 1, bands, dtype=F32)[None, :]
    z = jnp.concatenate([tt, jnp.cos(f * w), -jnp.sin(f * w)], axis=-1)
    zp = jnp.pad(z, ((0, 0), (0, LANES - emb)))
    w1 = jnp.pad(p["hf_w1"][l].astype(F32), ((0, LANES - emb), (0, 0)))
    cols = p["hf_w4"].shape[-1]
    tl = _tile(n, 256, SUBLANES)
    full = lambda shape: pl.BlockSpec(shape, lambda i: (0,) * len(shape))
    return pl.pallas_call(
        _filter_kernel,
        grid=(n // tl,),
        in_specs=[pl.BlockSpec((tl, LANES), lambda i: (i, 0)),
                  full((LANES, hid)), full((1, hid)), full((hid, hid)), full((1, hid)),
                  full((hid, hid)), full((1, hid)), full((hid, cols)), full((1, hid)), full((1, cols))],
        out_specs=pl.BlockSpec((tl, cols), lambda i: (i, 0)),
        out_shape=jax.ShapeDtypeStruct((n, cols), F32),
        compiler_params=_params(("parallel",)),
        name="hyena_filter_mlp",
    )(zp, w1, p["hf_b1"][l].astype(F32).reshape(1, hid), p["hf_w2"][l].astype(F32),
      p["hf_b2"][l].astype(F32).reshape(1, hid), p["hf_w3"][l].astype(F32),
      p["hf_b3"][l].astype(F32).reshape(1, hid), p["hf_w4"][l].astype(F32),
      p["hf_freq"][l].astype(F32).reshape(1, hid), p["hf_decay"][l].astype(F32).reshape(1, cols))


def _dft_tables(lseg, tq):
    n = 2 * lseg
    nfreq = lseg
    tb = 1
    while tb * tb < lseg:
        tb *= 2
    ta = lseg // tb
    k2 = 2 * jnp.arange(nfreq, dtype=jnp.int32)[:, None] + 1

    def cs(tvals):
        m = (k2 * tvals[None, :]) % (2 * n)
        ang = m.astype(F32) * (math.pi / n)
        return jnp.cos(ang), jnp.sin(ang)

    ca, sa = cs(jnp.arange(ta, dtype=jnp.int32) * tb)
    cb, sb = cs(jnp.arange(tb, dtype=jnp.int32))
    cos = (ca[:, :, None] * cb[:, None, :] - sa[:, :, None] * sb[:, None, :]).reshape(nfreq, lseg)
    nsin = -(sa[:, :, None] * cb[:, None, :] + ca[:, :, None] * sb[:, None, :]).reshape(nfreq, lseg)
    cos_b, nsin_b = cos.astype(BF16), nsin.astype(BF16)
    nq = nfreq // tq
    f_fwd = jnp.stack([cos_b.reshape(nq, tq, lseg), nsin_b.reshape(nq, tq, lseg)], axis=1).reshape(n, lseg)
    f_inv = (f_fwd.T.astype(F32) * (2.0 / n)).astype(BF16)
    return cos_b, nsin_b, f_fwd, f_inv


def _spec_mul_kernel(x_ref, kre_ref, kim_ref, o_ref, *, tq, nseg_prompt, ratio, nseq_sample):
    def cmul(xr, xi, kr, ki):
        return xr * kr - xi * ki, xr * ki + xi * kr

    def put(s, yr, yi):
        o_ref[s, 0:tq, :] = yr.astype(o_ref.dtype)
        o_ref[s, tq:2 * tq, :] = yi.astype(o_ref.dtype)

    for s in range(nseg_prompt):
        yr, yi = cmul(x_ref[s, 0:tq, :], x_ref[s, tq:2 * tq, :], kre_ref[0], kim_ref[0])
        put(s, yr, yi)
    for q in range(nseq_sample):
        base = nseg_prompt + q * ratio
        for a in range(ratio):
            acc_r = acc_i = None
            for b in range(ratio):
                kidx = 1 + (a - b) + (ratio - 1)
                yr, yi = cmul(x_ref[base + b, 0:tq, :], x_ref[base + b, tq:2 * tq, :], kre_ref[kidx], kim_ref[kidx])
                acc_r = yr if acc_r is None else acc_r + yr
                acc_i = yi if acc_i is None else acc_i + yi
            put(base + a, acc_r, acc_i)


def _spec_mul(xf, kre, kim, tq, nseg_prompt, ratio):
    nseg, n, w = xf.shape
    nspec = kre.shape[0]
    tc = _tile(w, 512, LANES)
    nseq_sample = (nseg - nseg_prompt) // ratio
    return pl.pallas_call(
        functools.partial(_spec_mul_kernel, tq=tq, nseg_prompt=nseg_prompt, ratio=ratio, nseq_sample=nseq_sample),
        grid=(n // (2 * tq), w // tc),
        in_specs=[pl.BlockSpec((nseg, 2 * tq, tc), lambda i, j: (0, i, j)),
                  pl.BlockSpec((nspec, tq, tc), lambda i, j: (0, i, j)),
                  pl.BlockSpec((nspec, tq, tc), lambda i, j: (0, i, j))],
        out_specs=pl.BlockSpec((nseg, 2 * tq, tc), lambda i, j: (0, i, j)),
        out_shape=jax.ShapeDtypeStruct(xf.shape, BF16),
        compiler_params=_params(("parallel", "parallel")),
        name="hyena_spectrum_mul",
    )(xf, kre, kim)


def _filter_sequences(p, l, lseg, ratio, w):
    e = jnp.arange(lseg, dtype=jnp.int32)
    order = p["hy_filter_bias"].shape[1]
    ls = ratio * lseg

    def at(pos, length):
        return _implicit_filters(pos, length, p, l).reshape(lseg, order, 2, w)

    up = [at(d * lseg + e, ls) for d in range(ratio)]
    down = [None] + [at(d * lseg - e, ls) for d in range(1, ratio)]
    not_first = (e > 0).astype(F32)[:, None]
    kp = at(e, lseg)
    pairs = [(kp[:, :, 0], kp[:, :, 1])]
    for d in range(-(ratio - 1), ratio):
        if d == 0:
            pairs.append((up[0][:, :, 0], up[0][:, :, 1]))
        elif d > 0:
            pairs.append((up[d][:, :, 0], down[d][:, :, 0]))
        else:
            pairs.append((down[-d][:, :, 1], up[-d][:, :, 1]))
    plus, minus = [], []
    for o in range(order):
        po, mo = [], []
        for idx, (c, a) in enumerate(pairs):
            co = c[:, o]
            ao = a[:, o] if idx in (0, ratio) else a[:, o] * not_first
            po.append((co + ao).astype(BF16))
            mo.append((co - ao).astype(BF16))
        plus.append(jnp.stack(po, axis=0))
        minus.append(jnp.stack(mo, axis=0))
    return jnp.stack(plus, axis=0), jnp.stack(minus, axis=0)


def _hyena_branch(x1, x2, v, p, l, nseg, lseg, nseg_prompt, ratio):
    ntok, w = v.shape
    n = 2 * lseg
    tq = _tile(lseg, 256, 16)
    cos_b, nsin_b, f_fwd, f_inv = _dft_tables(lseg, tq)
    plus, minus = _filter_sequences(p, l, lseg, ratio, w)
    order, nspec = plus.shape[:2]
    kre = _mm(cos_b, plus.reshape(order * nspec, lseg, w), name="hyena_filter_dft_re")
    kim = _mm(nsin_b, minus.reshape(order * nspec, lseg, w), name="hyena_filter_dft_im")
    kre = kre.reshape(order, nspec, lseg, w)
    kim = kim.reshape(order, nspec, lseg, w)
    fb = p["hy_filter_bias"][l].astype(F32)
    gates = (x1, x2)
    y = v
    for o in range(order):
        y3 = y.reshape(nseg, lseg, w)
        xf = _mm(f_fwd, y3.astype(BF16), name="hyena_dft_fwd")
        yf = _spec_mul(xf, kre[o], kim[o], tq, nseg_prompt, ratio)
        last = o == order - 1
        y = _mm(f_inv, yf,
                epilogue=lambda acc, gate, vin, bias: gate * (acc + vin * bias),
                extras=[(gates[o].reshape(nseg, lseg, w), "mn"), (y3, "mn"), (fb[o].reshape(1, w), "n")],
                out_dtypes=(BF16 if last else F32,), name="hyena_dft_inv").reshape(ntok, w)
    return y


def _router_kernel(h_ref, w_ref, b_ref, gate_ref, idx_ref, mask_ref):
    logits = jnp.dot(h_ref[...], w_ref[...], precision=HIGHEST, preferred_element_type=F32) + b_ref[...]
    lane = lax.broadcasted_iota(jnp.int32, logits.shape, 1).astype(F32)
    vals = logits
    tops, idxs = [], []
    mask = jnp.zeros(logits.shape, F32)
    for _ in range(TOP_K):
        m = jnp.max(vals, axis=-1, keepdims=True)
        idx = jnp.min(jnp.where(vals == m, lane, float(LANES)), axis=-1, keepdims=True)
        sel = lane == idx
        mask = jnp.where(sel, 1.0, mask)
        vals = jnp.where(sel, -jnp.inf, vals)
        tops.append(m)
        idxs.append(idx)
    es = [jnp.exp(t - tops[0]) for t in tops]
    denom = es[0]
    for e in es[1:]:
        denom = denom + e
    gate_c = jnp.zeros(logits.shape, F32)
    idx_c = jnp.zeros(logits.shape, F32)
    for k in range(TOP_K):
        gate_c = jnp.where(lane == k, es[k] / denom, gate_c)
        idx_c = jnp.where(lane == k, idxs[k], idx_c)
    gate_ref[...] = gate_c
    idx_ref[...] = idx_c.astype(jnp.int32)
    mask_ref[...] = mask.astype(BF16)


def _router(h, w_router, b_router):
    n, d = h.shape
    e = w_router.shape[-1]
    wp = jnp.pad(w_router.astype(F32), ((0, 0), (0, LANES - e)))
    bp = jnp.pad(b_router.astype(F32), (0, LANES - e), constant_values=NEG_BIG).reshape(1, LANES)
    tr = _tile(n, 256, 16)
    row = pl.BlockSpec((tr, LANES), lambda i: (i, 0))
    return pl.pallas_call(
        _router_kernel,
        grid=(n // tr,),
        in_specs=[pl.BlockSpec((tr, d), lambda i: (i, 0)),
                  pl.BlockSpec((d, LANES), lambda i: (0, 0)),
                  pl.BlockSpec((1, LANES), lambda i: (0, 0))],
        out_specs=[row, row, row],
        out_shape=[jax.ShapeDtypeStruct((n, LANES), F32), jax.ShapeDtypeStruct((n, LANES), jnp.int32),
                   jax.ShapeDtypeStruct((n, LANES), BF16)],
        compiler_params=_params(("parallel",)),
        name="moe_router",
    )(h, wp, bp)


def _rank_kernel(mask_ref, idx_ref, dest_ref, meta_ref, counts, carry, pstart, *, tr, block):
    ps = pl.program_id(0)
    i = pl.program_id(1)
    mask = mask_ref[...]
    colsum = jnp.sum(mask.astype(F32), axis=0, keepdims=True)

    @pl.when((ps == 0) & (i == 0))
    def _zero():
        counts[...] = jnp.zeros_like(counts)

    @pl.when(ps == 0)
    def _count():
        counts[...] += colsum
        dest_ref[...] = jnp.zeros_like(dest_ref)

    @pl.when((ps == 1) & (i == 0))
    def _starts():
        padded = jnp.floor((counts[...] + (block - 1)) / block) * block
        r = lax.broadcasted_iota(jnp.int32, (LANES, LANES), 0)
        c = lax.broadcasted_iota(jnp.int32, (LANES, LANES), 1)
        upper = (r < c).astype(F32)
        start = jnp.dot(jnp.broadcast_to(padded, (SUBLANES, LANES)), upper, precision=HIGHEST,
                        preferred_element_type=F32)[0:1, :]
        pstart[...] = start
        carry[...] = jnp.zeros_like(carry)
        rows = lax.broadcasted_iota(jnp.int32, (SUBLANES, LANES), 0)
        meta_ref[...] = jnp.where(rows == 0, start, jnp.where(rows == 1, start + padded, 0.0))

    @pl.when(ps == 1)
    def _rank():
        r = lax.broadcasted_iota(jnp.int32, (tr, tr), 0)
        c = lax.broadcasted_iota(jnp.int32, (tr, tr), 1)
        lower = (c < r).astype(BF16)
        rank = jnp.dot(lower, mask, preferred_element_type=F32)
        slot = pstart[...] + carry[...] + rank
        lane = lax.broadcasted_iota(jnp.int32, (tr, LANES), 1)
        idx_c = idx_ref[...]
        dest = jnp.zeros((tr, LANES), F32)
        for k in range(TOP_K):
            sel = lane == idx_c[:, k:k + 1]
            dk = jnp.sum(jnp.where(sel, slot, 0.0), axis=-1, keepdims=True)
            dest = jnp.where(lane == k, dk, dest)
        dest_ref[...] = dest.astype(jnp.int32)
        carry[...] += colsum


def _rank(mask, idx_c, block):
    n = mask.shape[0]
    tr = _tile(n, 256, 16)
    row = lambda: pl.BlockSpec((tr, LANES), lambda ps, i: (i, 0))
    return pl.pallas_call(
        functools.partial(_rank_kernel, tr=tr, block=block),
        grid=(2, n // tr),
        in_specs=[row(), row()],
        out_specs=[pl.BlockSpec((tr, LANES), lambda ps, i: (i * ps, 0)),
                   pl.BlockSpec((SUBLANES, LANES), lambda ps, i: (0, 0))],
        out_shape=[jax.ShapeDtypeStruct((n, LANES), jnp.int32), jax.ShapeDtypeStruct((SUBLANES, LANES), F32)],
        scratch_shapes=[pltpu.VMEM((1, LANES), F32), pltpu.VMEM((1, LANES), F32), pltpu.VMEM((1, LANES), F32)],
        compiler_params=_params(("arbitrary", "arbitrary")),
        name="moe_rank",
    )(mask, idx_c)


def _expert_kernel(be_ref, nbu_ref, tok_hbm, h_hbm, wg_ref, wl_ref, bg_ref, bl_ref, wd_ref, bd_ref,
                   out_ref, xf, x2, idx, sem_idx, sem_rows, *, tm, nb, tdn):
    b = pl.program_id(0)
    f = pl.program_id(1)
    d = x2.shape[1]

    def load_idx(blk):
        cp = pltpu.make_async_copy(tok_hbm.at[blk], idx, sem_idx)
        cp.start()
        cp.wait()

    def row_copy(tok, t):
        return pltpu.make_async_copy(h_hbm.at[pl.ds(tok, 1)], xf.at[pl.ds(t, 1)], sem_rows)

    def start_rows():
        def body(t, carry):
            row_copy(idx[t // LANES, t % LANES], t).start()
            return carry
        lax.fori_loop(0, tm, body, 0)

    def wait_rows():
        def body(t, carry):
            row_copy(0, t).wait()
            return carry
        lax.fori_loop(0, tm, body, 0)

    @pl.when((f == 0) & (b == 0))
    def _prime():
        load_idx(0)
        start_rows()

    @pl.when(f == 0)
    def _stage():
        wait_rows()
        x2[...] = xf[...].astype(BF16)

        @pl.when(b + 1 < nb)
        def _next():
            load_idx(b + 1)
            start_rows()

    used = b < nbu_ref[0]

    def compute(first):
        x = x2[...]
        g = jnp.dot(x, wg_ref[0], preferred_element_type=F32) + bg_ref[0]
        lin = jnp.dot(x, wl_ref[0], preferred_element_type=F32) + bl_ref[0]
        g = jnp.minimum(g, SWIGLU_LIMIT)
        lin = jnp.clip(lin, -SWIGLU_LIMIT, SWIGLU_LIMIT)
        act = (g * jax.nn.sigmoid(SWIGLU_ALPHA * g) * (lin + 1.0)).astype(BF16)
        for c in range(d // tdn):
            cols = slice(c * tdn, (c + 1) * tdn)
            part = jnp.dot(act, wd_ref[0, :, cols].astype(BF16), preferred_element_type=F32)
            if first:
                out_ref[:, cols] = part + bd_ref[0, :, cols]
            else:
                out_ref[:, cols] += part

    @pl.when(used & (f == 0))
    def _first():
        compute(True)

    @pl.when(used & (f != 0))
    def _rest():
        compute(False)

    @pl.when(jnp.logical_not(used) & (f == 0))
    def _unused():
        out_ref[...] = jnp.zeros_like(out_ref)


def _experts(h, slot_token3, block_expert, nb_used, w_gu, b_gu, w_down, b_down, tm, tf):
    nb = slot_token3.shape[0]
    d = h.shape[1]
    e, _, f2 = w_gu.shape
    ff = f2 // 2
    nf = ff // tf
    tdn = _tile(d, 512, LANES)

    def fidx(b, f, nbu):
        return jnp.where(b < nbu[0], f, nf - 1)

    def eidx(b, be):
        return be[b]

    grid_spec = pltpu.PrefetchScalarGridSpec(
        num_scalar_prefetch=2,
        grid=(nb, nf),
        in_specs=[
            pl.BlockSpec(memory_space=pl.ANY),
            pl.BlockSpec(memory_space=pl.ANY),
            pl.BlockSpec((1, d, tf), lambda b, f, be, nbu: (eidx(b, be), 0, fidx(b, f, nbu))),
            pl.BlockSpec((1, d, tf), lambda b, f, be, nbu: (eidx(b, be), 0, nf + fidx(b, f, nbu))),
            pl.BlockSpec((1, 1, tf), lambda b, f, be, nbu: (eidx(b, be), 0, fidx(b, f, nbu))),
            pl.BlockSpec((1, 1, tf), lambda b, f, be, nbu: (eidx(b, be), 0, nf + fidx(b, f, nbu))),
            pl.BlockSpec((1, tf, d), lambda b, f, be, nbu: (eidx(b, be), fidx(b, f, nbu), 0)),
            pl.BlockSpec((1, 1, d), lambda b, f, be, nbu: (eidx(b, be), 0, 0)),
        ],
        out_specs=pl.BlockSpec((tm, d), lambda b, f, be, nbu: (b, 0)),
        scratch_shapes=[
            pltpu.VMEM((tm, d), F32),
            pltpu.VMEM((tm, d), BF16),
            pltpu.SMEM(slot_token3.shape[1:], jnp.int32),
            pltpu.SemaphoreType.DMA,
            pltpu.SemaphoreType.DMA,
        ],
    )
    return pl.pallas_call(
        functools.partial(_expert_kernel, tm=tm, nb=nb, tdn=tdn),
        grid_spec=grid_spec,
        out_shape=jax.ShapeDtypeStruct((nb * tm, d), F32),
        compiler_params=_params(("arbitrary", "arbitrary")),
        name="moe_experts",
    )(block_expert, nb_used, slot_token3, h, w_gu, w_gu, b_gu, b_gu, w_down, b_down)


def _combine_kernel(dest_hbm, o_hbm, gate_ref, h_ref, g_ref, b_ref, y_ref, buf, pre, idx, sem_idx, sem_rows,
                    *, td, ntile, tdn, alpha):
    i = pl.program_id(0)
    cur = i % 2
    d = pre.shape[1]

    def row_copy(row, s, k, t):
        return pltpu.make_async_copy(o_hbm.at[pl.ds(row, 1)], buf.at[s, k, pl.ds(t, 1)], sem_rows.at[s])

    def fetch(tile, s):
        cp = pltpu.make_async_copy(dest_hbm.at[tile], idx.at[s], sem_idx)
        cp.start()
        cp.wait()

        def body(t, carry):
            for k in range(TOP_K):
                flat = t * TOP_K + k
                row_copy(idx[s, flat // LANES, flat % LANES], s, k, t).start(priority=k % 2)
            return carry

        lax.fori_loop(0, td, body, 0)

    @pl.when(i == 0)
    def _prime():
        fetch(0, 0)

    @pl.when(i + 1 < ntile)
    def _next():
        fetch(i + 1, 1 - cur)

    def wait_body(t, carry):
        for k in range(TOP_K):
            row_copy(0, cur, k, t).wait()
        return carry

    lax.fori_loop(0, td, wait_body, 0)
    gates = gate_ref[...]
    gk = [gates[:, k:k + 1] for k in range(TOP_K)]
    for c in range(d // tdn):
        cols = slice(c * tdn, (c + 1) * tdn)
        moe = gk[0] * buf[cur, 0, :, cols]
        for k in range(1, TOP_K):
            moe = moe + gk[k] * buf[cur, k, :, cols]
        pre[:, cols] = alpha * h_ref[:, cols] + moe
    y_ref[...] = _ln_rows(pre[...], g_ref[...], b_ref[...])


def _combine_ln(dest3, out2, gate_c, h, ln_g, ln_b, alpha):
    n, d = h.shape
    ntile = dest3.shape[0]
    td = n // ntile
    tdn = _tile(d, 512, LANES)
    return pl.pallas_call(
        functools.partial(_combine_kernel, td=td, ntile=ntile, tdn=tdn, alpha=alpha),
        grid=(ntile,),
        in_specs=[
            pl.BlockSpec(memory_space=pl.ANY),
            pl.BlockSpec(memory_space=pl.ANY),
            pl.BlockSpec((td, LANES), lambda i: (i, 0)),
            pl.BlockSpec((td, d), lambda i: (i, 0)),
            pl.BlockSpec((1, d), lambda i: (0, 0)),
            pl.BlockSpec((1, d), lambda i: (0, 0)),
        ],
        out_specs=pl.BlockSpec((td, d), lambda i: (i, 0)),
        out_shape=jax.ShapeDtypeStruct((n, d), F32),
        scratch_shapes=[
            pltpu.VMEM((2, TOP_K, td, d), F32),
            pltpu.VMEM((td, d), F32),
            pltpu.SMEM((2,) + dest3.shape[1:], jnp.int32),
            pltpu.SemaphoreType.DMA,
            pltpu.SemaphoreType.DMA((2,)),
        ],
        compiler_params=_params(("arbitrary",)),
        name="moe_combine_ln",
    )(dest3, out2, gate_c, h, ln_g.reshape(1, d), ln_b.reshape(1, d))


def _moe_ln(h, p, l, alpha, w_gu_b, w_down_b):
    n, d = h.shape
    e = p["mo_w_router"].shape[-1]
    tm = min(512, max(LANES, n * TOP_K // 8))
    td = _tile(n, 128, 32)
    ff = p["mo_w_down"].shape[2]
    tf = _tile(ff, 256, LANES)
    gate_c, idx_c, mask = _router(h, p["mo_w_router"][l], p["mo_b_router"][l])
    dest_c, meta = _rank(mask, idx_c, tm)
    nb = -(-(n * TOP_K) // tm) + e
    n_slots = nb * tm
    dest = dest_c[:, :TOP_K].reshape(-1)
    token_flat = jnp.arange(n * TOP_K, dtype=jnp.int32) // TOP_K
    slot_token = jnp.zeros((n_slots,), jnp.int32).at[dest].set(token_flat)
    pend = meta[1, :e].astype(jnp.int32)
    block_start = jnp.arange(nb, dtype=jnp.int32) * tm
    nb_used = pend[e - 1] // tm
    block_expert = jnp.minimum(jnp.searchsorted(pend, block_start, side="right"), e - 1).astype(jnp.int32)
    last_expert = block_expert[jnp.maximum(nb_used - 1, 0)]
    block_expert = jnp.where(block_start < pend[e - 1], block_expert, last_expert)
    out3 = _experts(h, slot_token.reshape(nb, tm // LANES, LANES), block_expert, nb_used.reshape(1),
                    w_gu_b, p["mo_b_gu"][l].astype(F32).reshape(e, 1, -1),
                    w_down_b, p["mo_b_down"][l].astype(F32).reshape(e, 1, -1), tm, tf)
    dest3 = dest.reshape(n // td, td * TOP_K // LANES, LANES)
    return _combine_ln(dest3, out3, gate_c, h, p["ln2_g"][l].astype(F32), p["ln2_b"][l].astype(F32), alpha)


def _sigmoid_gate(acc, gate):
    return jax.nn.sigmoid(gate) * acc


def _sigmoid_gate_add(acc, prev, gate):
    return prev + jax.nn.sigmoid(gate) * acc


def _glu(acc, y, bias):
    return y.astype(F32) * jax.nn.sigmoid(acc + bias)


def kernel(x_prompt, x_sample, ln0_g, ln0_b, w_in, ssm_A_re, ssm_A_im, ssm_log_step, ssm_B_re, ssm_B_im, ssm_C_re, ssm_C_im, ssm_D, ssm_w_glu, ssm_b_glu, hy_short_w, hy_short_b, hf_w1, hf_b1, hf_w2, hf_b2, hf_w3, hf_b3, hf_w4, hf_freq, hf_decay, hy_filter_bias, w_branch_a, w_branch_b, w_out, ln1_g, ln1_b, mo_w_router, mo_b_router, mo_w_gu, mo_b_gu, mo_w_down, mo_b_down, ln2_g, ln2_b):
    p = dict(ssm_A_re=ssm_A_re, ssm_A_im=ssm_A_im, ssm_log_step=ssm_log_step, ssm_B_re=ssm_B_re,
             ssm_B_im=ssm_B_im, ssm_C_re=ssm_C_re, ssm_C_im=ssm_C_im, ssm_D=ssm_D,
             hf_w1=hf_w1, hf_b1=hf_b1, hf_w2=hf_w2, hf_b2=hf_b2, hf_w3=hf_w3, hf_b3=hf_b3, hf_w4=hf_w4,
             hf_freq=hf_freq, hf_decay=hf_decay, hy_filter_bias=hy_filter_bias,
             mo_w_router=mo_w_router, mo_b_router=mo_b_router, mo_b_gu=mo_b_gu, mo_w_down=mo_w_down,
             mo_b_down=mo_b_down, ln2_g=ln2_g, ln2_b=ln2_b)
    bp, lp, d = x_prompt.shape
    bs, ls, _ = x_sample.shape
    depth = w_in.shape[0]
    alpha = (2.0 * depth) ** 0.25
    lseg = min(lp, ls)
    assert lp == lseg and ls % lseg == 0, "sample sequences must be whole multiples of the prompt length"
    ratio = ls // lseg
    nseg_prompt = bp
    nseg = bp + bs * ratio
    assert nseg == SUBLANES, "the S5 scan keeps one segment per sublane"
    n_prompt = bp * lp
    ssm_w = ssm_D.shape[-1]
    hy_w = hy_filter_bias.shape[-1]
    c1 = ssm_w
    c2 = c1 + 3 * hy_w
    c3 = c2 + d

    x_all = jnp.concatenate([x_prompt.reshape(-1, d), x_sample.reshape(-1, d)], axis=0)
    h, hb = _layer_norm(x_all, ln0_g.astype(F32), ln0_b.astype(F32))
    for l in range(depth):
        w_in_b = w_in[l].astype(BF16)
        u_a = _mm(hb, w_in_b[:, :c1], out_dtypes=(BF16,), name="proj_s5")
        u_b = _mm(hb, w_in_b[:, c1:c2], name="proj_hyena")
        g_a = _mm(hb, w_in_b[:, c2:c3], name="proj_gate_a")
        g_b = _mm(hb, w_in_b[:, c3:], name="proj_gate_b")

        y_s5 = _s5_branch(u_a, p, l, nseg, lseg, nseg_prompt, ratio)
        s5_out = _mm(y_s5, ssm_w_glu[l].astype(BF16), epilogue=_glu,
                     extras=[(y_s5, "mn"), (ssm_b_glu[l].astype(F32).reshape(1, -1), "n")],
                     out_dtypes=(BF16,), name="s5_glu")

        sw = hy_short_w[l].astype(F32)
        sb = hy_short_b[l].astype(F32)
        x1, x2, v = [_shortconv(u_b, sw, sb, part, hy_w, n_prompt, lp, ls) for part in range(3)]
        y_hy = _hyena_branch(x1, x2, v, p, l, nseg, lseg, nseg_prompt, ratio)

        m_a = _mm(s5_out, w_branch_a[l].astype(BF16), epilogue=_sigmoid_gate, extras=[(g_a, "mn")],
                  name="branch_a")
        merged = _mm(y_hy, w_branch_b[l].astype(BF16), epilogue=_sigmoid_gate_add,
                     extras=[(m_a, "mn"), (g_b, "mn")], out_dtypes=(BF16,), name="branch_b_merge")
        mix = _mm(merged, w_out[l].astype(BF16), name="mixer_out")
        h, hb = _layer_norm(mix, ln1_g[l].astype(F32), ln1_b[l].astype(F32), res=h, alpha=alpha)
        h = _moe_ln(h, p, l, alpha, mo_w_gu[l].astype(BF16), mo_w_down[l])
        if l + 1 < depth:
            hb = h.astype(BF16)
    y_prompt = h[:n_prompt].reshape(bp, lp, d)
    y_sample = h[n_prompt:].reshape(bs, ls, d)
    return (y_prompt, y_sample)
```
